```python
import math
import jax
import jax.numpy as jnp
from jax import lax
import numpy as np

D_MODEL = 4096
BATCH = 2
SEQ = 4096
DEPTH = 2

GRID_W = 64
CTX_LEN = 256
N_GROUPS = 4
GROUP_W = D_MODEL // N_GROUPS
FNET_GROUPS = 4
FNET_GW = GROUP_W // FNET_GROUPS
HYENA_ORDER = 2
HYENA_BANDS = 16
HYENA_EMB = 2 * HYENA_BANDS + 1
HYENA_HID = 64
HYENA_FAST_DECAY = math.log(1e-2) / 0.3
HYENA_SLOW_DECAY = math.log(1e-2) / 1.5
SHORT_CONV = 3
MLSTM_HEADS = 4
MLSTM_DH = GROUP_W // MLSTM_HEADS
GLA_HEADS = 4
GLA_DV = GROUP_W // GLA_HEADS
GLA_DK = GLA_DV // 2
GLA_RANK = 16
GLA_TAU = 16.0
N_DIR = 2
CHUNK = 64
EPS = 1e-6
F32 = jnp.float32

PROJ_LAYOUT = (
    ("a_u", GROUP_W), ("a_g", GROUP_W),
    ("b_u", 3 * GROUP_W), ("b_g", GROUP_W),
    ("c_q", GROUP_W), ("c_k", GROUP_W), ("c_v", GROUP_W), ("c_o", GROUP_W), ("c_g", GROUP_W),
    ("c_if", N_DIR * 2 * MLSTM_HEADS),
    ("d_q", GLA_HEADS * GLA_DK), ("d_k", GLA_HEADS * GLA_DK), ("d_v", GROUP_W), ("d_g", GROUP_W),
    ("d_a", N_DIR * GLA_RANK),
)
PROJ_W = sum(w for _, w in PROJ_LAYOUT)
CTX_STATE_NAMES = ("c_k", "c_v", "c_if", "d_k", "d_v", "d_a")

kernel_name = "hybrid_parallel_mixer_dit"


def rmsnorm(x, g):
    xf = x.astype(F32)
    y = xf * lax.rsqrt(jnp.mean(xf * xf, axis=-1, keepdims=True) + EPS)
    return (y * g.astype(F32)).astype(x.dtype)


def project(h, w_in, names):
    slices, off = {}, 0
    for name, w in PROJ_LAYOUT:
        slices[name] = (off, off + w)
        off += w
    if names is None:
        z = h @ w_in
        return {n: z[..., a:b] for n, (a, b) in slices.items()}
    return {n: h @ w_in[:, slices[n][0]:slices[n][1]] for n in names}


def short_conv(x, w, b):
    xp = jnp.pad(x, ((0, 0), (1, 1), (0, 0)))
    return xp[:, :-2] * w[0] + xp[:, 1:-1] * w[1] + xp[:, 2:] * w[2] + b


def to_col_major(a, rows):
    b, l, c = a.shape
    return a.reshape(b, rows, GRID_W, c).transpose(0, 2, 1, 3).reshape(b, l, c)


def from_col_major(a, rows):
    b, l, c = a.shape
    return a.reshape(b, GRID_W, rows, c).transpose(0, 2, 1, 3).reshape(b, l, c)


def merge_heads(o):
    b, h, l, d = o.shape
    return o.transpose(0, 2, 1, 3).reshape(b, l, h * d)


def flip_t(a):
    return None if a is None else jnp.flip(a, axis=2)


def fourier_mix(u, w):
    b, l, _ = u.shape
    uf = u.astype(F32).reshape(b, l, FNET_GROUPS, FNET_GW)
    y = jnp.fft.fftn(uf, axes=(1, 3), norm="ortho").real.reshape(b, l, GROUP_W)
    return y @ w.astype(F32)


def hyena_filter_spectrum(l, p):
    t = jnp.linspace(0.0, 1.0, l, dtype=F32)[:, None]
    bands = jnp.linspace(1e-4, HYENA_BANDS - 1, HYENA_BANDS, dtype=F32)[None, :]
    ang = (2.0 * math.pi / l) * jnp.arange(l, dtype=F32)[:, None] * bands
    feats = jnp.concatenate([t, jnp.cos(ang), -jnp.sin(ang)], axis=-1)
    fr = p["hy_f_freq"].astype(F32)
    hid = jnp.sin(fr[0] * (feats @ p["hy_f_w1"].astype(F32) + p["hy_f_b1"].astype(F32)))
    hid = jnp.sin(fr[1] * (hid @ p["hy_f_w2"].astype(F32) + p["hy_f_b2"].astype(F32)))
    filt = (hid @ p["hy_f_w3"].astype(F32)).reshape(l, HYENA_ORDER, N_DIR, GROUP_W)
    deltas = jnp.abs(jnp.linspace(HYENA_SLOW_DECAY, HYENA_FAST_DECAY, GROUP_W, dtype=F32))
    filt = filt * jnp.exp(-t * deltas)[:, None, None, :]
    fwd, bwd = filt[:, :, 0], filt[:, :, 1]
    k = jnp.concatenate([fwd, jnp.zeros_like(fwd[:1]), bwd[:0:-1]], axis=0)
    k = k / (jnp.sum(jnp.abs(k), axis=0, keepdims=True) + EPS)
    return jnp.fft.rfft(k, axis=0)


def long_conv(z, k_hat, bias):
    l = z.shape[1]
    y = jnp.fft.irfft(jnp.fft.rfft(z, n=2 * l, axis=1) * k_hat, n=2 * l, axis=1)[:, :l]
    return y + bias * z


def hyena_mix(u, conv_w, conv_b, k_hat, bias):
    u = short_conv(u, conv_w, conv_b).astype(F32)
    v, x1, x2 = jnp.split(u, 3, axis=-1)
    bias = bias.astype(F32)
    z = x1 * long_conv(v, k_hat[:, 0], bias[0])
    return x2 * long_conv(z, k_hat[:, 1], bias[1])


def mlstm_prep(q, k, v, gif, conv_w, conv_b, gate_b):
    b, l, _ = k.shape

    def heads(a):
        return a.astype(F32).reshape(b, l, MLSTM_HEADS, MLSTM_DH).transpose(0, 2, 1, 3)
    if q is not None:
        q = heads(jax.nn.silu(short_conv(q, conv_w[0], conv_b[0])))
    k = heads(jax.nn.silu(short_conv(k, conv_w[1], conv_b[1]))) * (MLSTM_DH ** -0.5)
    v = heads(v)
    g = gif.astype(F32).reshape(b, l, N_DIR, 2, MLSTM_HEADS) + gate_b.astype(F32)
    g = jnp.transpose(g, (2, 3, 0, 4, 1))
    return q, k, v, g[:, 0], jax.nn.log_sigmoid(g[:, 1])


def mlstm_zero_state(b):
    return (jnp.zeros((b, MLSTM_HEADS, MLSTM_DH, MLSTM_DH), F32),
            jnp.zeros((b, MLSTM_HEADS, MLSTM_DH), F32),
            jnp.zeros((b, MLSTM_HEADS), F32))


def mlstm_scan(q, k, v, li, lf, state):
    b, h, l, d = k.shape
    nc = l // CHUNK

    def chunked(a):
        return None if a is None else jnp.moveaxis(a.reshape(b, h, nc, CHUNK, *a.shape[3:]), 2, 0)
    mask = jnp.tril(jnp.ones((CHUNK, CHUNK), dtype=bool))

    def step(carry, xs):
        c_st, n_st, m_st = carry
        qc, kc, vc, lic, lfc = xs
        bcum = jnp.cumsum(lfc, axis=-1)
        logw = jnp.where(mask, bcum[..., :, None] - bcum[..., None, :] + lic[..., None, :], -jnp.inf)
        m_pos = jnp.maximum(bcum + m_st[..., None], jnp.max(logw, axis=-1))
        m_next = m_pos[..., -1]
        w_state = jnp.exp(logw[..., -1, :] - m_next[..., None])
        decay = jnp.exp(bcum[..., -1] + m_st - m_next)
        c_next = decay[..., None, None] * c_st + jnp.einsum("bhs,bhsk,bhsv->bhkv", w_state, kc, vc)
        n_next = decay[..., None] * n_st + jnp.einsum("bhs,bhsk->bhk", w_state, kc)
        if qc is None:
            return (c_next, n_next, m_next), None
        w_intra = jnp.exp(logw - m_pos[..., None])
        w_prev = jnp.exp(bcum + m_st[..., None] - m_pos)
        s = jnp.einsum("bhtk,bhsk->bhts", qc, kc) * w_intra
        num = jnp.einsum("bhts,bhsv->bhtv", s, vc) + w_prev[..., None] * jnp.einsum("bhtk,bhkv->bhtv", qc, c_st)
        den = jnp.sum(s, axis=-1) + w_prev * jnp.einsum("bhtk,bhk->bht", qc, n_st)
        out = num / jnp.maximum(jnp.abs(den), jnp.exp(-m_pos))[..., None]
        return (c_next, n_next, m_next), out

    xs = (chunked(q), chunked(k), chunked(v), chunked(li), chunked(lf))
    state, out = lax.scan(step, state, xs)
    if out is not None:
        out = jnp.moveaxis(out, 0, 2).reshape(b, h, l, d)
    return out, state


def gla_prep(q, k, v, a_low, w2, b2):
    b, l, _ = k.shape

    def heads(a, dh):
        return a.astype(F32).reshape(b, l, GLA_HEADS, dh).transpose(0, 2, 1, 3)
    if q is not None:
        q = heads(q, GLA_DK) * (GLA_DK ** -0.5)
    k = heads(k, GLA_DK)
    v = heads(v, GLA_DV)
    a = a_low.astype(F32).reshape(b, l, N_DIR, GLA_RANK)
    la = jax.nn.log_sigmoid(jnp.einsum("bldr,drk->dblk", a, w2.astype(F32)) + b2.astype(F32)[:, None, None, :]) / GLA_TAU
    la = la.reshape(N_DIR, b, l, GLA_HEADS, GLA_DK).transpose(0, 1, 3, 2, 4)
    return q, k, v, la


def gla_zero_state(b):
    return jnp.zeros((b, GLA_HEADS, GLA_DK, GLA_DV), F32)


def gla_scan(q, k, v, la, state):
    b, h, l, dk = k.shape
    nc = l // CHUNK

    def chunked(a):
        return None if a is None else jnp.moveaxis(a.reshape(b, h, nc, CHUNK, *a.shape[3:]), 2, 0)
    mask = jnp.tril(jnp.ones((CHUNK, CHUNK), dtype=bool))[:, :, None]

    def step(s_prev, xs):
        qc, kc, vc, lac = xs
        bc = jnp.cumsum(lac, axis=2)
        b_last = bc[:, :, -1]
        s_next = jnp.exp(b_last)[..., None] * s_prev + jnp.einsum(
            "bhsk,bhsv->bhkv", kc * jnp.exp(b_last[:, :, None] - bc), vc)
        if qc is None:
            return s_next, None
        decay = jnp.exp(jnp.where(mask, bc[:, :, :, None] - bc[:, :, None], -jnp.inf))
        att = jnp.einsum("bhtk,bhsk,bhtsk->bhts", qc, kc, decay)
        out = jnp.einsum("bhts,bhsv->bhtv", att, vc) + jnp.einsum("bhtk,bhkv->bhtv", qc * jnp.exp(bc), s_prev)
        return s_next, out

    state, out = lax.scan(step, state, (chunked(q), chunked(k), chunked(v), chunked(la)))
    if out is not None:
        out = jnp.moveaxis(out, 0, 2).reshape(b, h, l, GLA_DV)
    return out, state


def gla_out(o, g):
    o = o * lax.rsqrt(jnp.mean(o * o, axis=-1, keepdims=True) + EPS)
    return merge_heads(o) * g.astype(F32)


def bidir(scan_fn, q, k, v, gates, states):
    out_f, st_f = scan_fn(q, k, v, *[g[0] for g in gates], states[0])
    out_b, st_b = scan_fn(flip_t(q), flip_t(k), flip_t(v), *[flip_t(g[1]) for g in gates], states[1])
    out = None if q is None else out_f + flip_t(out_b)
    return out, (st_f, st_b)


def combine(z, o_m, o_g, p):
    l = z["a_u"].shape[1]

    def silu(a):
        return jax.nn.silu(a.astype(F32))
    y_a = fourier_mix(z["a_u"], p["fnet_w"]) * silu(z["a_g"])
    y_b = hyena_mix(z["b_u"], p["hy_conv_w"], p["hy_conv_b"], hyena_filter_spectrum(l, p), p["hy_bias"]) * silu(z["b_g"])
    y_c = o_m * jax.nn.sigmoid(z["c_o"].astype(F32)) * silu(z["c_g"])
    y_d = gla_out(o_g, p["gla_norm_g"]) * silu(z["d_g"])
    y = jnp.concatenate([y_a, y_b, y_c, y_d], axis=-1).astype(z["a_u"].dtype)
    return y @ p["w_out"]


def mixer_layer(x, xc, c, c_ctx, p, last):
    b, l, d = x.shape
    rows = l // GRID_W
    mod = jax.nn.silu(c) @ p["ada_w"] + p["ada_b"]
    shift, scale, gate = jnp.split(mod, 3, axis=-1)
    h = rmsnorm(x, p["norm_g"]) * (1 + scale[:, None]) + shift[:, None]
    n_mod = 2 if last else 3
    mod_c = jax.nn.silu(c_ctx) @ p["ada_w"][:, :n_mod * d] + p["ada_b"][:n_mod * d]
    hc = rmsnorm(xc, p["norm_g"]) * (1 + mod_c[d:2 * d]) + mod_c[:d]
    z = project(h, p["w_in"], None)
    zc = project(hc, p["w_in"], CTX_STATE_NAMES if last else None)
    ctx_q_m = None if last else zc["c_q"]
    ctx_q_g = None if last else zc["d_q"]

    mc = mlstm_prep(ctx_q_m, zc["c_k"], zc["c_v"], zc["c_if"], p["ml_conv_w"], p["ml_conv_b"], p["ml_gate_b"])
    ctx_m, m_states = bidir(mlstm_scan, mc[0], mc[1], mc[2], mc[3:], (mlstm_zero_state(b), mlstm_zero_state(b)))
    ml = mlstm_prep(to_col_major(z["c_q"], rows), to_col_major(z["c_k"], rows), to_col_major(z["c_v"], rows),
                    to_col_major(z["c_if"], rows), p["ml_conv_w"], p["ml_conv_b"], p["ml_gate_b"])
    lat_m, _ = bidir(mlstm_scan, ml[0], ml[1], ml[2], ml[3:], m_states)
    lat_m = from_col_major(merge_heads(lat_m), rows)

    gc = gla_prep(ctx_q_g, zc["d_k"], zc["d_v"], zc["d_a"], p["gla_w2"], p["gla_b2"])
    ctx_g, g_states = bidir(gla_scan, gc[0], gc[1], gc[2], gc[3:], (gla_zero_state(b), gla_zero_state(b)))
    gl = gla_prep(z["d_q"], z["d_k"], z["d_v"], z["d_a"], p["gla_w2"], p["gla_b2"])
    lat_g, _ = bidir(gla_scan, gl[0], gl[1], gl[2], gl[3:], g_states)

    x = x + gate[:, None] * combine(z, lat_m, lat_g, p)
    if not last:
        xc = xc + mod_c[2 * d:] * combine(zc, merge_heads(ctx_m), ctx_g, p)
    return x, xc


def setup_inputs(seed: int = 0) -> dict:
    key = jax.random.key(seed)
    ks = jax.random.split(key, 32)

    def nrm(i, shape, s):
        return jax.random.normal(ks[i], shape, F32) * s
    D, W, L = D_MODEL, GROUP_W, DEPTH
    f_bias = jnp.linspace(3.0, 6.0, MLSTM_HEADS, dtype=F32)
    ml_gate_b = jnp.concatenate([nrm(20, (L, N_DIR, 1, MLSTM_HEADS), 0.1),
                                 f_bias + nrm(21, (L, N_DIR, 1, MLSTM_HEADS), 0.1)], axis=2)
    return {
        "x": nrm(0, (BATCH, SEQ, D), 1.0),
        "c": nrm(1, (BATCH, D), 1.0),
        "ctx": nrm(2, (BATCH, CTX_LEN, D), 1.0),
        "c_ctx": nrm(3, (D,), 1.0),
        "ada_w": nrm(4, (L, D, 3 * D), 0.5 * D ** -0.5),
        "ada_b": nrm(5, (L, 3 * D), 0.02),
        "norm_g": 1.0 + nrm(6, (L, D), 0.02),
        "w_in": nrm(7, (L, D, PROJ_W), D ** -0.5),
        "fnet_w": nrm(8, (L, W, W), W ** -0.5),
        "hy_conv_w": nrm(9, (L, SHORT_CONV, 3 * W), SHORT_CONV ** -0.5),
        "hy_conv_b": nrm(10, (L, 3 * W), 0.02),
        "hy_f_w1": nrm(11, (L, HYENA_EMB, HYENA_HID), HYENA_EMB ** -0.5),
        "hy_f_b1": nrm(12, (L, HYENA_HID), 0.02),
        "hy_f_w2": nrm(13, (L, HYENA_HID, HYENA_HID), HYENA_HID ** -0.5),
        "hy_f_b2": nrm(14, (L, HYENA_HID), 0.02),
        "hy_f_w3": nrm(15, (L, HYENA_HID, HYENA_ORDER * N_DIR * W), HYENA_HID ** -0.5),
        "hy_f_freq": 1.0 + nrm(16, (L, 2, HYENA_HID), 0.02),
        "hy_bias": nrm(17, (L, HYENA_ORDER, W), 0.1),
        "ml_conv_w": nrm(18, (L, 2, SHORT_CONV, W), SHORT_CONV ** -0.5),
        "ml_conv_b": nrm(19, (L, 2, W), 0.02),
        "ml_gate_b": ml_gate_b,
        "gla_w2": nrm(22, (L, N_DIR, GLA_RANK, GLA_HEADS * GLA_DK), GLA_RANK ** -0.5),
        "gla_b2": nrm(23, (L, N_DIR, GLA_HEADS * GLA_DK), 0.02),
        "gla_norm_g": 1.0 + nrm(24, (L, W), 0.02),
        "w_out": nrm(25, (L, D, D), D ** -0.5),
        "final_g": 1.0 + nrm(26, (D,), 0.02),
    }


def reference(x, c, ctx, c_ctx, ada_w, ada_b, norm_g, w_in, fnet_w, hy_conv_w, hy_conv_b, hy_f_w1, hy_f_b1,
              hy_f_w2, hy_f_b2, hy_f_w3, hy_f_freq, hy_bias, ml_conv_w, ml_conv_b, ml_gate_b, gla_w2, gla_b2,
              gla_norm_g, w_out, final_g):
    xc = ctx
    for i in range(DEPTH):
        p = {
            "ada_w": ada_w[i], "ada_b": ada_b[i], "norm_g": norm_g[i], "w_in": w_in[i], "fnet_w": fnet_w[i],
            "hy_conv_w": hy_conv_w[i], "hy_conv_b": hy_conv_b[i], "hy_f_w1": hy_f_w1[i], "hy_f_b1": hy_f_b1[i],
            "hy_f_w2": hy_f_w2[i], "hy_f_b2": hy_f_b2[i], "hy_f_w3": hy_f_w3[i], "hy_f_freq": hy_f_freq[i],
            "hy_bias": hy_bias[i], "ml_conv_w": ml_conv_w[i], "ml_conv_b": ml_conv_b[i], "ml_gate_b": ml_gate_b[i],
            "gla_w2": gla_w2[i], "gla_b2": gla_b2[i], "gla_norm_g": gla_norm_g[i], "w_out": w_out[i],
        }
        x, xc = mixer_layer(x, xc, c, c_ctx, p, i == DEPTH - 1)
    return rmsnorm(x, final_g)
```

```python
import functools
import math

import jax
import jax.numpy as jnp
from jax import lax
from jax.experimental import pallas as pl
from jax.experimental.pallas import tpu as pltpu

D_MODEL = 4096
BATCH = 2
SEQ = 4096
DEPTH = 2
GRID_W = 64
CTX_LEN = 256
N_GROUPS = 4
GROUP_W = D_MODEL // N_GROUPS
FNET_GROUPS = 4
FNET_GW = GROUP_W // FNET_GROUPS
HYENA_ORDER = 2
HYENA_BANDS = 16
HYENA_EMB = 2 * HYENA_BANDS + 1
HYENA_HID = 64
HYENA_FAST_DECAY = math.log(1e-2) / 0.3
HYENA_SLOW_DECAY = math.log(1e-2) / 1.5
SHORT_CONV = 3
MLSTM_HEADS = 4
MLSTM_DH = GROUP_W // MLSTM_HEADS
GLA_HEADS = 4
GLA_DV = GROUP_W // GLA_HEADS
GLA_DK = GLA_DV // 2
GLA_RANK = 16
GLA_TAU = 16.0
N_DIR = 2
CHUNK = 64
EPS = 1e-6
F32 = jnp.float32
BF16 = jnp.bfloat16

PROJ_LAYOUT = (
    ("a_u", GROUP_W), ("a_g", GROUP_W),
    ("b_u", 3 * GROUP_W), ("b_g", GROUP_W),
    ("c_q", GROUP_W), ("c_k", GROUP_W), ("c_v", GROUP_W), ("c_o", GROUP_W), ("c_g", GROUP_W),
    ("c_if", N_DIR * 2 * MLSTM_HEADS),
    ("d_q", GLA_HEADS * GLA_DK), ("d_k", GLA_HEADS * GLA_DK), ("d_v", GROUP_W), ("d_g", GROUP_W),
    ("d_a", N_DIR * GLA_RANK),
)
PROJ_W = sum(w for _, w in PROJ_LAYOUT)
CTX_STATE_NAMES = ("c_k", "c_v", "c_if", "d_k", "d_v", "d_a")

V7X_LANES = 128
V7X_VMEM_LIMIT_BYTES = 56 * 1024 * 1024


def _matmul_body(a_ref, b_ref, o_ref):
    o_ref[...] = jnp.dot(a_ref[...], b_ref[...], preferred_element_type=F32)


def _pick_tile(n, cands):
    for c in cands:
        if n % c == 0:
            return c
    raise ValueError(f"no tile for {n}")


def pallas_matmul(a, b):
    m, k = a.shape
    k2, n = b.shape
    assert k == k2
    n_pad = -(-n // 512) * 512
    if n_pad != n:
        b = jnp.pad(b, ((0, 0), (0, n_pad - n)))
    m_pad = -(-m // 8) * 8
    if m_pad != m:
        a = jnp.pad(a, ((0, m_pad - m), (0, 0)))
    tm = _pick_tile(m_pad, (1024, 512, 256, 128, 64, 32, 16, 8))
    tn = 512
    out = pl.pallas_call(
        _matmul_body,
        grid=(m_pad // tm, n_pad // tn),
        in_specs=[pl.BlockSpec((tm, k), lambda i, j: (i, 0)),
                  pl.BlockSpec((k, tn), lambda i, j: (0, j))],
        out_specs=pl.BlockSpec((tm, tn), lambda i, j: (i, j)),
        out_shape=jax.ShapeDtypeStruct((m_pad, n_pad), F32),
        compiler_params=pltpu.CompilerParams(
            dimension_semantics=("arbitrary", "arbitrary"),
            vmem_limit_bytes=V7X_VMEM_LIMIT_BYTES),
        name="proj_matmul",
    )(a.astype(BF16), b.astype(BF16))
    return out[:m, :n]


def mm(a, b):
    lead = a.shape[:-1]
    out = pallas_matmul(a.reshape(-1, a.shape[-1]), b)
    return out.reshape(*lead, b.shape[-1])


def rmsnorm(x, g):
    xf = x.astype(F32)
    y = xf * lax.rsqrt(jnp.mean(xf * xf, axis=-1, keepdims=True) + EPS)
    return (y * g.astype(F32)).astype(x.dtype)


def project(h, w_in, names):
    slices, off = {}, 0
    for name, w in PROJ_LAYOUT:
        slices[name] = (off, off + w)
        off += w
    if names is None:
        z = mm(h, w_in)
        return {n: z[..., a:b] for n, (a, b) in slices.items()}
    w_sel = jnp.concatenate([w_in[:, slices[n][0]:slices[n][1]] for n in names], axis=1)
    z = mm(h, w_sel)
    out, off = {}, 0
    for n in names:
        w = slices[n][1] - slices[n][0]
        out[n] = z[..., off:off + w]
        off += w
    return out


def short_conv(x, w, b):
    xp = jnp.pad(x, ((0, 0), (1, 1), (0, 0)))
    return xp[:, :-2] * w[0] + xp[:, 1:-1] * w[1] + xp[:, 2:] * w[2] + b


def to_col_major(a, rows):
    b, l, c = a.shape
    return a.reshape(b, rows, GRID_W, c).transpose(0, 2, 1, 3).reshape(b, l, c)


def from_col_major(a, rows):
    b, l, c = a.shape
    return a.reshape(b, GRID_W, rows, c).transpose(0, 2, 1, 3).reshape(b, l, c)


def merge_heads(o):
    b, h, l, d = o.shape
    return o.transpose(0, 2, 1, 3).reshape(b, l, h * d)


def flip_t(a):
    return None if a is None else jnp.flip(a, axis=2)


def fourier_mix(u, w):
    b, l, _ = u.shape
    uf = u.astype(F32).reshape(b, l, FNET_GROUPS, FNET_GW)
    y = jnp.fft.fftn(uf, axes=(1, 3), norm="ortho").real.reshape(b, l, GROUP_W)
    return y @ w.astype(F32)


def hyena_filter_spectrum(l, p):
    t = jnp.linspace(0.0, 1.0, l, dtype=F32)[:, None]
    bands = jnp.linspace(1e-4, HYENA_BANDS - 1, HYENA_BANDS, dtype=F32)[None, :]
    ang = (2.0 * math.pi / l) * jnp.arange(l, dtype=F32)[:, None] * bands
    feats = jnp.concatenate([t, jnp.cos(ang), -jnp.sin(ang)], axis=-1)
    fr = p["hy_f_freq"].astype(F32)
    hid = jnp.sin(fr[0] * (feats @ p["hy_f_w1"].astype(F32) + p["hy_f_b1"].astype(F32)))
    hid = jnp.sin(fr[1] * (hid @ p["hy_f_w2"].astype(F32) + p["hy_f_b2"].astype(F32)))
    filt = (hid @ p["hy_f_w3"].astype(F32)).reshape(l, HYENA_ORDER, N_DIR, GROUP_W)
    deltas = jnp.abs(jnp.linspace(HYENA_SLOW_DECAY, HYENA_FAST_DECAY, GROUP_W, dtype=F32))
    filt = filt * jnp.exp(-t * deltas)[:, None, None, :]
    fwd, bwd = filt[:, :, 0], filt[:, :, 1]
    k = jnp.concatenate([fwd, jnp.zeros_like(fwd[:1]), bwd[:0:-1]], axis=0)
    k = k / (jnp.sum(jnp.abs(k), axis=0, keepdims=True) + EPS)
    return jnp.fft.rfft(k, axis=0)


def long_conv(z, k_hat, bias):
    l = z.shape[1]
    y = jnp.fft.irfft(jnp.fft.rfft(z, n=2 * l, axis=1) * k_hat, n=2 * l, axis=1)[:, :l]
    return y + bias * z


def hyena_mix(u, conv_w, conv_b, k_hat, bias):
    u = short_conv(u, conv_w, conv_b).astype(F32)
    v, x1, x2 = jnp.split(u, 3, axis=-1)
    bias = bias.astype(F32)
    z = x1 * long_conv(v, k_hat[:, 0], bias[0])
    return x2 * long_conv(z, k_hat[:, 1], bias[1])


def mlstm_prep(q, k, v, gif, conv_w, conv_b, gate_b):
    b, l, _ = k.shape

    def heads(a):
        return a.astype(F32).reshape(b, l, MLSTM_HEADS, MLSTM_DH).transpose(0, 2, 1, 3)
    if q is not None:
        q = heads(jax.nn.silu(short_conv(q, conv_w[0], conv_b[0])))
    k = heads(jax.nn.silu(short_conv(k, conv_w[1], conv_b[1]))) * (MLSTM_DH ** -0.5)
    v = heads(v)
    g = gif.astype(F32).reshape(b, l, N_DIR, 2, MLSTM_HEADS) + gate_b.astype(F32)
    g = jnp.transpose(g, (2, 3, 0, 4, 1))
    return q, k, v, g[:, 0], jax.nn.log_sigmoid(g[:, 1])


def mlstm_zero_state(b):
    return (jnp.zeros((b, MLSTM_HEADS, MLSTM_DH, MLSTM_DH), F32),
            jnp.zeros((b, MLSTM_HEADS, MLSTM_DH), F32),
            jnp.zeros((b, MLSTM_HEADS), F32))


def mlstm_scan(q, k, v, li, lf, state):
    b, h, l, d = k.shape
    nc = l // CHUNK

    def chunked(a):
        return None if a is None else jnp.moveaxis(a.reshape(b, h, nc, CHUNK, *a.shape[3:]), 2, 0)
    mask = jnp.tril(jnp.ones((CHUNK, CHUNK), dtype=bool))

    def step(carry, xs):
        c_st, n_st, m_st = carry
        qc, kc, vc, lic, lfc = xs
        bcum = jnp.cumsum(lfc, axis=-1)
        logw = jnp.where(mask, bcum[..., :, None] - bcum[..., None, :] + lic[..., None, :], -jnp.inf)
        m_pos = jnp.maximum(bcum + m_st[..., None], jnp.max(logw, axis=-1))
        m_next = m_pos[..., -1]
        w_state = jnp.exp(logw[..., -1, :] - m_next[..., None])
        decay = jnp.exp(bcum[..., -1] + m_st - m_next)
        c_next = decay[..., None, None] * c_st + jnp.einsum("bhs,bhsk,bhsv->bhkv", w_state, kc, vc)
        n_next = decay[..., None] * n_st + jnp.einsum("bhs,bhsk->bhk", w_state, kc)
        if qc is None:
            return (c_next, n_next, m_next), None
        w_intra = jnp.exp(logw - m_pos[..., None])
        w_prev = jnp.exp(bcum + m_st[..., None] - m_pos)
        s = jnp.einsum("bhtk,bhsk->bhts", qc, kc) * w_intra
        num = jnp.einsum("bhts,bhsv->bhtv", s, vc) + w_prev[..., None] * jnp.einsum("bhtk,bhkv->bhtv", qc, c_st)
        den = jnp.sum(s, axis=-1) + w_prev * jnp.einsum("bhtk,bhk->bht", qc, n_st)
        out = num / jnp.maximum(jnp.abs(den), jnp.exp(-m_pos))[..., None]
        return (c_next, n_next, m_next), out

    xs = (chunked(q), chunked(k), chunked(v), chunked(li), chunked(lf))
    state, out = lax.scan(step, state, xs)
    if out is not None:
        out = jnp.moveaxis(out, 0, 2).reshape(b, h, l, d)
    return out, state


def gla_prep(q, k, v, a_low, w2, b2):
    b, l, _ = k.shape

    def heads(a, dh):
        return a.astype(F32).reshape(b, l, GLA_HEADS, dh).transpose(0, 2, 1, 3)
    if q is not None:
        q = heads(q, GLA_DK) * (GLA_DK ** -0.5)
    k = heads(k, GLA_DK)
    v = heads(v, GLA_DV)
    a = a_low.astype(F32).reshape(b, l, N_DIR, GLA_RANK)
    la = jax.nn.log_sigmoid(jnp.einsum("bldr,drk->dblk", a, w2.astype(F32)) + b2.astype(F32)[:, None, None, :]) / GLA_TAU
    la = la.reshape(N_DIR, b, l, GLA_HEADS, GLA_DK).transpose(0, 1, 3, 2, 4)
    return q, k, v, la


def gla_zero_state(b):
    return jnp.zeros((b, GLA_HEADS, GLA_DK, GLA_DV), F32)


def gla_scan(q, k, v, la, state):
    b, h, l, dk = k.shape
    nc = l // CHUNK

    def chunked(a):
        return None if a is None else jnp.moveaxis(a.reshape(b, h, nc, CHUNK, *a.shape[3:]), 2, 0)
    mask = jnp.tril(jnp.ones((CHUNK, CHUNK), dtype=bool))[:, :, None]

    def step(s_prev, xs):
        qc, kc, vc, lac = xs
        bc = jnp.cumsum(lac, axis=2)
        b_last = bc[:, :, -1]
        s_next = jnp.exp(b_last)[..., None] * s_prev + jnp.einsum(
            "bhsk,bhsv->bhkv", kc * jnp.exp(b_last[:, :, None] - bc), vc)
        if qc is None:
            return s_next, None
        decay = jnp.exp(jnp.where(mask, bc[:, :, :, None] - bc[:, :, None], -jnp.inf))
        att = jnp.einsum("bhtk,bhsk,bhtsk->bhts", qc, kc, decay)
        out = jnp.einsum("bhts,bhsv->bhtv", att, vc) + jnp.einsum("bhtk,bhkv->bhtv", qc * jnp.exp(bc), s_prev)
        return s_next, out

    state, out = lax.scan(step, state, (chunked(q), chunked(k), chunked(v), chunked(la)))
    if out is not None:
        out = jnp.moveaxis(out, 0, 2).reshape(b, h, l, GLA_DV)
    return out, state


def gla_out(o, g):
    o = o * lax.rsqrt(jnp.mean(o * o, axis=-1, keepdims=True) + EPS)
    return merge_heads(o) * g.astype(F32)


def bidir(scan_fn, q, k, v, gates, states):
    out_f, st_f = scan_fn(q, k, v, *[g[0] for g in gates], states[0])
    out_b, st_b = scan_fn(flip_t(q), flip_t(k), flip_t(v), *[flip_t(g[1]) for g in gates], states[1])
    out = None if q is None else out_f + flip_t(out_b)
    return out, (st_f, st_b)


def combine(z, o_m, o_g, p):
    l = z["a_u"].shape[1]

    def silu(a):
        return jax.nn.silu(a.astype(F32))
    y_a = fourier_mix(z["a_u"], p["fnet_w"]) * silu(z["a_g"])
    y_b = hyena_mix(z["b_u"], p["hy_conv_w"], p["hy_conv_b"], hyena_filter_spectrum(l, p), p["hy_bias"]) * silu(z["b_g"])
    y_c = o_m * jax.nn.sigmoid(z["c_o"].astype(F32)) * silu(z["c_g"])
    y_d = gla_out(o_g, p["gla_norm_g"]) * silu(z["d_g"])
    y = jnp.concatenate([y_a, y_b, y_c, y_d], axis=-1).astype(z["a_u"].dtype)
    return mm(y, p["w_out"])


def mixer_layer(x, xc, c, c_ctx, p, last):
    b, l, d = x.shape
    rows = l // GRID_W
    mod = mm(jax.nn.silu(c), p["ada_w"]) + p["ada_b"]
    shift, scale, gate = jnp.split(mod, 3, axis=-1)
    h = rmsnorm(x, p["norm_g"]) * (1 + scale[:, None]) + shift[:, None]
    n_mod = 2 if last else 3
    mod_c = mm(jax.nn.silu(c_ctx)[None], p["ada_w"][:, :n_mod * d])[0] + p["ada_b"][:n_mod * d]
    hc = rmsnorm(xc, p["norm_g"]) * (1 + mod_c[d:2 * d]) + mod_c[:d]
    z = project(h, p["w_in"], None)
    zc = project(hc, p["w_in"], CTX_STATE_NAMES if last else None)
    ctx_q_m = None if last else zc["c_q"]
    ctx_q_g = None if last else zc["d_q"]

    mc = mlstm_prep(ctx_q_m, zc["c_k"], zc["c_v"], zc["c_if"], p["ml_conv_w"], p["ml_conv_b"], p["ml_gate_b"])
    ctx_m, m_states = bidir(mlstm_scan, mc[0], mc[1], mc[2], mc[3:], (mlstm_zero_state(b), mlstm_zero_state(b)))
    ml = mlstm_prep(to_col_major(z["c_q"], rows), to_col_major(z["c_k"], rows), to_col_major(z["c_v"], rows),
                    to_col_major(z["c_if"], rows), p["ml_conv_w"], p["ml_conv_b"], p["ml_gate_b"])
    lat_m, _ = bidir(mlstm_scan, ml[0], ml[1], ml[2], ml[3:], m_states)
    lat_m = from_col_major(merge_heads(lat_m), rows)

    gc = gla_prep(ctx_q_g, zc["d_k"], zc["d_v"], zc["d_a"], p["gla_w2"], p["gla_b2"])
    ctx_g, g_states = bidir(gla_scan, gc[0], gc[1], gc[2], gc[3:], (gla_zero_state(b), gla_zero_state(b)))
    gl = gla_prep(z["d_q"], z["d_k"], z["d_v"], z["d_a"], p["gla_w2"], p["gla_b2"])
    lat_g, _ = bidir(gla_scan, gl[0], gl[1], gl[2], gl[3:], g_states)

    x = x + gate[:, None] * combine(z, lat_m, lat_g, p)
    if not last:
        xc = xc + mod_c[2 * d:] * combine(zc, merge_heads(ctx_m), ctx_g, p)
    return x, xc


def kernel(x, c, ctx, c_ctx, ada_w, ada_b, norm_g, w_in, fnet_w, hy_conv_w, hy_conv_b, hy_f_w1, hy_f_b1,
           hy_f_w2, hy_f_b2, hy_f_w3, hy_f_freq, hy_bias, ml_conv_w, ml_conv_b, ml_gate_b, gla_w2, gla_b2,
           gla_norm_g, w_out, final_g):
    xc = ctx
    for i in range(DEPTH):
        p = {
            "ada_w": ada_w[i], "ada_b": ada_b[i], "norm_g": norm_g[i], "w_in": w_in[i], "fnet_w": fnet_w[i],
            "hy_conv_w": hy_conv_w[i], "hy_conv_b": hy_conv_b[i], "hy_f_w1": hy_f_w1[i], "hy_f_b1": hy_f_b1[i],
            "hy_f_w2": hy_f_w2[i], "hy_f_b2": hy_f_b2[i], "hy_f_w3": hy_f_w3[i], "hy_f_freq": hy_f_freq[i],
            "hy_bias": hy_bias[i], "ml_conv_w": ml_conv_w[i], "ml_conv_b": ml_conv_b[i], "ml_gate_b": ml_gate_b[i],
            "gla_w2": gla_w2[i], "gla_b2": gla_b2[i], "gla_norm_g": gla_norm_g[i], "w_out": w_out[i],
        }
        x, xc = mixer_layer(x, xc, c, c_ctx, p, i == DEPTH - 1)
    return rmsnorm(x, final_g)
```

```python
import functools
import math

import numpy as np
import jax
import jax.numpy as jnp
from jax import lax
from jax.experimental import pallas as pl
from jax.experimental.pallas import tpu as pltpu

D_MODEL = 4096
GRID_W = 64
N_GROUPS = 4
GROUP_W = D_MODEL // N_GROUPS
FNET_GROUPS = 4
FNET_GW = GROUP_W // FNET_GROUPS
HYENA_ORDER = 2
HYENA_BANDS = 16
HYENA_FAST_DECAY = math.log(1e-2) / 0.3
HYENA_SLOW_DECAY = math.log(1e-2) / 1.5
MLSTM_HEADS = 4
MLSTM_DH = GROUP_W // MLSTM_HEADS
GLA_HEADS = 4
GLA_DV = GROUP_W // GLA_HEADS
GLA_DK = GLA_DV // 2
GLA_RANK = 16
GLA_TAU = 16.0
N_DIR = 2
CHUNK = 64
EPS = 1e-6
F32 = jnp.float32
BF16 = jnp.bfloat16

W = GROUP_W
OFF = {"a_u": 0, "a_g": W, "b_u": 2 * W, "b_g": 5 * W, "c_q": 6 * W, "c_k": 7 * W, "c_v": 8 * W,
       "c_o": 9 * W, "c_g": 10 * W, "d_q": 11 * W, "d_k": 11 * W + 512, "d_v": 12 * W, "d_g": 13 * W}
ZB_W = 14 * W
ZS_W = 128
N_IF = N_DIR * 2 * MLSTM_HEADS
N_DA = N_DIR * GLA_RANK
ORIG_IF = 11 * W
ORIG_DA = ORIG_IF + N_IF + 2 * 512 + 2 * W

V7X_VMEM_LIMIT_BYTES = 56 * 1024 * 1024
SUBLANES = 8


def _cparams(n_axes):
    return pltpu.CompilerParams(dimension_semantics=("arbitrary",) * n_axes,
                                vmem_limit_bytes=V7X_VMEM_LIMIT_BYTES)


def _split3(a):
    hi = a.astype(BF16)
    r1 = a - hi.astype(F32)
    mid = r1.astype(BF16)
    lo = (r1 - mid.astype(F32)).astype(BF16)
    return hi, mid, lo


def _dot(a, b):
    return jnp.dot(a, b, preferred_element_type=F32)


def _dot_nt(a, b):
    return lax.dot_general(a, b, (((1,), (1,)), ((), ())), preferred_element_type=F32)


def _dot_tn(a, b):
    return lax.dot_general(a, b, (((0,), (0,)), ((), ())), preferred_element_type=F32)


def _dot_exact_lhs(m_bf16, x):
    hi, mid, lo = _split3(x)
    return _dot(m_bf16, hi) + _dot(m_bf16, mid) + _dot(m_bf16, lo)


def _dot_exact_rhs(x, m_bf16):
    hi, mid, lo = _split3(x)
    return _dot(hi, m_bf16) + _dot(mid, m_bf16) + _dot(lo, m_bf16)


def _dot_f32x3(a, b):
    ah = a.astype(BF16)
    al = (a - ah.astype(F32)).astype(BF16)
    bh = b.astype(BF16)
    bl = (b - bh.astype(F32)).astype(BF16)
    return _dot(ah, bh) + _dot(ah, bl) + _dot(al, bh)


def _log_sigmoid(x):
    return jnp.minimum(x, 0.0) - jnp.log1p(jnp.exp(-jnp.abs(x)))


def _silu(x):
    return x * jax.nn.sigmoid(x)


def _row_iota(shape):
    return lax.broadcasted_iota(jnp.int32, shape, 0)


def _shift_down(x, d):
    n = x.shape[0]
    if d % SUBLANES == 0:
        return jnp.concatenate([jnp.zeros((d,) + x.shape[1:], x.dtype), x[: n - d]], axis=0)
    return jnp.where(_row_iota(x.shape) >= d, pltpu.roll(x, d, 0), 0.0)


def _shift_up(x, d):
    n = x.shape[0]
    if d == 0:
        return x
    if d % SUBLANES == 0:
        return jnp.concatenate([x[d:], jnp.zeros((d,) + x.shape[1:], x.dtype)], axis=0)
    return jnp.where(_row_iota(x.shape) < n - d, pltpu.roll(x, n - d, 0), 0.0)


def _conv3_silu(x_ref, prev_ref, next_ref, w_ref, tap0, bias_row, is_first, is_last):
    x = x_ref[0].astype(F32)
    n = x.shape[0]
    rows = _row_iota(x.shape)
    prev_row = jnp.where(is_first, 0.0, prev_ref[0][SUBLANES - 1:SUBLANES, :].astype(F32))
    next_row = jnp.where(is_last, 0.0, next_ref[0][0:1, :].astype(F32))
    xm = jnp.where(rows == 0, prev_row, pltpu.roll(x, 1, 0))
    xp = jnp.where(rows == n - 1, next_row, pltpu.roll(x, n - 1, 0))
    y = (xm * w_ref[tap0:tap0 + 1, :] + x * w_ref[tap0 + 1:tap0 + 2, :] + xp * w_ref[tap0 + 2:tap0 + 3, :]
         + w_ref[bias_row:bias_row + 1, :])
    return _silu(y)


def _mlstm_body(nc, *refs):
    (qf, kf, vf, qfp, qfn, kfp, kfn, gcf, grf,
     qb, kb, vb, qbp, qbn, kbp, kbn, gcb, grb,
     cw_ref, gbc_ref, gbr_ref, tri_ref, c0_ref, n0_ref, m0_ref,
     of_ref, ob_ref, cN_ref, nN_ref, mN_ref, c_sc, n_sc, m_sc) = refs
    i = pl.program_id(1)

    @pl.when(i == 0)
    def _():
        c_sc[...] = c0_ref[0]
        n_sc[...] = n0_ref[0]
        m_sc[...] = m0_ref[0]

    dh = MLSTM_DH
    for d in range(N_DIR):
        q_ref, k_ref, v_ref, qp, qn, kp, kn, gc_ref, gr_ref, o_ref = (
            (qf, kf, vf, qfp, qfn, kfp, kfn, gcf, grf, of_ref) if d == 0 else
            (qb, kb, vb, qbp, qbn, kbp, kbn, gcb, grb, ob_ref))
        chunk = i if d == 0 else nc - 1 - i
        is_first, is_last = chunk == 0, chunk == nc - 1
        q = _conv3_silu(q_ref, qp, qn, cw_ref, 0, 6, is_first, is_last)
        k = _conv3_silu(k_ref, kp, kn, cw_ref, 3, 7, is_first, is_last) * (dh ** -0.5)
        v = v_ref[0]
        tri = tri_ref[d]
        tri_t = tri_ref[1 - d]
        gcol = gc_ref[0] + gbc_ref[...]
        grow = gr_ref[0, 0] + gbr_ref[...]
        bcum_c = _dot_exact_lhs(tri, _log_sigmoid(gcol))
        bcum_r = _dot_exact_rhs(_log_sigmoid(grow), tri_t)
        mask = tri.astype(F32) > 0.5
        last = CHUNK - 1 if d == 0 else 0
        q16, k16 = q.astype(BF16), k.astype(BF16)
        outs = []
        for h in range(MLSTM_HEADS):
            ci, cf = d * 8 + h, d * 8 + MLSTM_HEADS + h
            bc = bcum_c[:, cf:cf + 1]
            li_c = gcol[:, ci:ci + 1]
            br = bcum_r[cf:cf + 1, :]
            li_r = grow[ci:ci + 1, :]
            m_st = m_sc[d, h][0:1, 0:1]
            logw = jnp.where(mask, bc - (br - li_r), -jnp.inf)
            m_pos = jnp.maximum(bc + m_st, jnp.max(logw, axis=1, keepdims=True))
            w_intra = jnp.exp(logw - m_pos)
            w_prev = jnp.exp(bc + m_st - m_pos)
            sl = slice(h * dh, (h + 1) * dh)
            qh, kh, vh = q16[:, sl], k16[:, sl], v[:, sl]
            c_st = c_sc[d, h]
            n_st = n_sc[d, h][0:1, :]
            s = _dot_nt(qh, kh) * w_intra
            num = _dot(s.astype(BF16), vh) + w_prev * _dot(qh, c_st.astype(BF16))
            den = jnp.sum(s, axis=1, keepdims=True) + w_prev * jnp.sum(q[:, sl] * n_st, axis=1, keepdims=True)
            outs.append(num / jnp.maximum(jnp.abs(den), jnp.exp(-m_pos)))
            m_next = m_pos[last:last + 1, :]
            bc_last = bc[last:last + 1, :]
            w_state = jnp.exp(bc_last - bc + li_c - m_next)
            decay = jnp.exp(bc_last + m_st - m_next)
            kw = k[:, sl] * w_state
            c_sc[d, h] = decay * c_st + _dot_tn(kw.astype(BF16), vh)
            n_sc[d, h] = jnp.broadcast_to(decay * n_st + jnp.sum(kw, axis=0, keepdims=True), (SUBLANES, dh))
            m_sc[d, h] = jnp.broadcast_to(m_next, (SUBLANES, 128))
        o_ref[0] = jnp.concatenate(outs, axis=1).astype(o_ref.dtype)

    @pl.when(i == nc - 1)
    def _():
        cN_ref[0] = c_sc[...]
        nN_ref[0] = n_sc[...]
        mN_ref[0] = m_sc[...]


def _tri_consts():
    t = np.tril(np.ones((CHUNK, CHUNK), np.float32))
    return jnp.asarray(np.stack([t, t.T]), BF16)


def mlstm_zero_state(b):
    return (jnp.zeros((b, N_DIR, MLSTM_HEADS, MLSTM_DH, MLSTM_DH), F32),
            jnp.zeros((b, N_DIR, MLSTM_HEADS, SUBLANES, MLSTM_DH), F32),
            jnp.zeros((b, N_DIR, MLSTM_HEADS, SUBLANES, 128), F32))


def mlstm_scan(zb, zs, conv_w, conv_b, gate_b, state, col_major):
    b, l, _ = zb.shape
    nc = l // CHUNK
    nblk = ZB_W // W
    oq, ok, ov = OFF["c_q"] // W, OFF["c_k"] // W, OFF["c_v"] // W
    cw8 = jnp.concatenate([conv_w.reshape(6, W), conv_b.reshape(2, W)], axis=0).astype(F32)
    gb = gate_b.astype(F32).reshape(N_IF)
    gbc = jnp.zeros((1, ZS_W), F32).at[0, :N_IF].set(gb)
    gbr = jnp.broadcast_to(gb[:, None], (N_IF, CHUNK))
    if col_major:
        rows = l // GRID_W
        assert rows == CHUNK and GRID_W == nc
        zb_v = zb.reshape(b, rows, GRID_W * ZB_W)
        zs_v = zs.reshape(b, rows, GRID_W * ZS_W)
        grow = zs.reshape(b, rows, GRID_W, ZS_W)[..., :N_IF].transpose(0, 2, 3, 1)
        last_rb = rows // SUBLANES - 1

        def main(off, dirn):
            return pl.BlockSpec((1, CHUNK, W), lambda bi, i: (bi, 0, _ch(i, dirn) * nblk + off))

        def prev(off, dirn):
            return pl.BlockSpec((1, SUBLANES, W),
                                lambda bi, i: (bi, last_rb, jnp.maximum(_ch(i, dirn) - 1, 0) * nblk + off))

        def nxt(off, dirn):
            return pl.BlockSpec((1, SUBLANES, W),
                                lambda bi, i: (bi, 0, jnp.minimum(_ch(i, dirn) + 1, nc - 1) * nblk + off))

        def gcol(dirn):
            return pl.BlockSpec((1, CHUNK, ZS_W), lambda bi, i: (bi, 0, _ch(i, dirn)))

        def out_spec(dirn):
            return pl.BlockSpec((1, CHUNK, W), lambda bi, i: (bi, 0, _ch(i, dirn)))
        out_shape_o = jax.ShapeDtypeStruct((b, rows, GRID_W * W), BF16)
    else:
        zb_v, zs_v = zb, zs
        grow = zs.reshape(b, nc, CHUNK, ZS_W)[..., :N_IF].transpose(0, 1, 3, 2)
        cpb = CHUNK // SUBLANES

        def main(off, dirn):
            return pl.BlockSpec((1, CHUNK, W), lambda bi, i: (bi, _ch(i, dirn), off))

        def prev(off, dirn):
            return pl.BlockSpec((1, SUBLANES, W),
                                lambda bi, i: (bi, jnp.maximum(_ch(i, dirn) * cpb - 1, 0), off))

        def nxt(off, dirn):
            return pl.BlockSpec((1, SUBLANES, W),
                                lambda bi, i: (bi, jnp.minimum((_ch(i, dirn) + 1) * cpb, nc * cpb - 1), off))

        def gcol(dirn):
            return pl.BlockSpec((1, CHUNK, ZS_W), lambda bi, i: (bi, _ch(i, dirn), 0))

        def out_spec(dirn):
            return pl.BlockSpec((1, CHUNK, W), lambda bi, i: (bi, _ch(i, dirn), 0))
        out_shape_o = jax.ShapeDtypeStruct((b, l, W), BF16)

    def _ch(i, dirn):
        return i if dirn == 0 else nc - 1 - i

    def grow_spec(dirn):
        return pl.BlockSpec((1, 1, N_IF, CHUNK), lambda bi, i: (bi, _ch(i, dirn), 0, 0))

    def full(shape):
        return pl.BlockSpec(shape, lambda bi, i: (0,) * len(shape))

    def per_b(shape):
        return pl.BlockSpec((1,) + shape, lambda bi, i: (bi,) + (0,) * len(shape))

    in_specs, args = [], []
    for dirn in range(N_DIR):
        in_specs += [main(oq, dirn), main(ok, dirn), main(ov, dirn), prev(oq, dirn), nxt(oq, dirn),
                     prev(ok, dirn), nxt(ok, dirn), gcol(dirn), grow_spec(dirn)]
        args += [zb_v, zb_v, zb_v, zb_v, zb_v, zb_v, zb_v, zs_v, grow]
    c_shape = (N_DIR, MLSTM_HEADS, MLSTM_DH, MLSTM_DH)
    n_shape = (N_DIR, MLSTM_HEADS, SUBLANES, MLSTM_DH)
    m_shape = (N_DIR, MLSTM_HEADS, SUBLANES, 128)
    in_specs += [full((8, W)), full((1, ZS_W)), full((N_IF, CHUNK)), full((2, CHUNK, CHUNK)),
                 per_b(c_shape), per_b(n_shape), per_b(m_shape)]
    args += [cw8, gbc, gbr, _tri_consts(), *state]
    out_f, out_b, c_n, n_n, m_n = pl.pallas_call(
        functools.partial(_mlstm_body, nc),
        grid=(b, nc),
        in_specs=in_specs,
        out_specs=[out_spec(0), out_spec(1), per_b(c_shape), per_b(n_shape), per_b(m_shape)],
        out_shape=[out_shape_o, out_shape_o,
                   jax.ShapeDtypeStruct((b,) + c_shape, F32), jax.ShapeDtypeStruct((b,) + n_shape, F32),
                   jax.ShapeDtypeStruct((b,) + m_shape, F32)],
        scratch_shapes=[pltpu.VMEM(c_shape, F32), pltpu.VMEM(n_shape, F32), pltpu.VMEM(m_shape, F32)],
        compiler_params=_cparams(2),
        name="mlstm_scan",
    )(*args)
    return out_f.reshape(b, l, W), out_b.reshape(b, l, W), (c_n, n_n, m_n)


GLA_LEVELS = (1, 2, 4, 8, 16, 32)


def _gla_masks():
    t = np.arange(CHUNK)[:, None]
    s = np.arange(CHUNK)[None, :]
    ms = [(t == s)]
    for m in GLA_LEVELS:
        ms.append((t // (2 * m) == s // (2 * m)) & (t % (2 * m) >= m) & (s % (2 * m) < m))
    return jnp.asarray(np.stack(ms).astype(np.float32))


def _cumsum_rows(x):
    d = 1
    while d < x.shape[0]:
        x = x + _shift_down(x, d)
        d *= 2
    return x


def _prev_block_end(x, m):
    n, w = x.shape
    if m >= SUBLANES:
        ends = x.reshape(n // m, m, w)[:, m - 1:m, :]
        prev = jnp.concatenate([jnp.zeros((1, 1, w), x.dtype), ends[:-1]], axis=0)
        return jnp.broadcast_to(prev, (n // m, m, w)).reshape(n, w)
    r = _row_iota(x.shape) % m
    y = _shift_down(x, 1)
    for j in range(1, m):
        y = jnp.where(r == j, _shift_down(x, j + 1), y)
    return y


def _own_block_end(x, m):
    n, w = x.shape
    if m >= SUBLANES:
        ends = x.reshape(n // m, m, w)[:, m - 1:m, :]
        return jnp.broadcast_to(ends, (n // m, m, w)).reshape(n, w)
    r = _row_iota(x.shape) % m
    y = x
    for j in range(m - 1):
        y = jnp.where(r == j, _shift_up(x, m - 1 - j), y)
    return y


def _gla_body(nc, with_out, *refs):
    (qf, kf, vf, af, qb, kb, vb, ab, w2_ref, b2_ref, j_ref, mask_ref, s0_ref,
     of_ref, ob_ref, sN_ref, s_sc) = refs
    i = pl.program_id(1)

    @pl.when(i == 0)
    def _():
        s_sc[...] = s0_ref[0]

    dk, dv = GLA_DK, GLA_DV
    for d in range(N_DIR):
        q_ref, k_ref, v_ref, a_ref, o_ref = (qf, kf, vf, af, of_ref) if d == 0 else (qb, kb, vb, ab, ob_ref)
        q16, k16, v16, a = q_ref[0], k_ref[0], v_ref[0], a_ref[0]
        if d == 1:
            jm = j_ref[...]
            q16 = _dot(jm, q16).astype(BF16)
            k16 = _dot(jm, k16).astype(BF16)
            v16 = _dot(jm, v16).astype(BF16)
            a = _dot_exact_lhs(jm, a)
        la = _log_sigmoid(_dot_f32x3(a, w2_ref[d]) + b2_ref[d]) * (1.0 / GLA_TAU)
        bc = _cumsum_rows(la)
        q = q16.astype(F32) * (dk ** -0.5)
        k = k16.astype(F32)
        b_last = bc[CHUNK - 1:CHUNK, :]
        q_in = (q * jnp.exp(bc)).astype(BF16)
        k_dec = (k * jnp.exp(b_last - bc)).astype(BF16)
        e_last = jnp.exp(b_last)
        if with_out:
            qs, ks = [q.astype(BF16)], [k16]
            for m in GLA_LEVELS:
                qs.append((q * jnp.exp(bc - _prev_block_end(bc, m))).astype(BF16))
                ks.append((k * jnp.exp(_own_block_end(bc, m) - bc)).astype(BF16))
        outs = []
        for h in range(GLA_HEADS):
            sk = slice(h * dk, (h + 1) * dk)
            sv = slice(h * dv, (h + 1) * dv)
            st = s_sc[d, h]
            if with_out:
                att = jnp.zeros((CHUNK, CHUNK), F32)
                for lvl in range(len(GLA_LEVELS) + 1):
                    att = att + mask_ref[lvl] * _dot_nt(qs[lvl][:, sk], ks[lvl][:, sk])
                outs.append(_dot(att.astype(BF16), v16[:, sv]) + _dot_nt(q_in[:, sk], st.astype(BF16)))
            s_sc[d, h] = st * e_last[:, sk] + _dot_tn(v16[:, sv], k_dec[:, sk])
        if with_out:
            o = jnp.concatenate(outs, axis=1).astype(BF16)
            if d == 1:
                o = _dot(j_ref[...], o).astype(BF16)
            o_ref[0] = o
        else:
            o_ref[0] = jnp.zeros(o_ref.shape[1:], o_ref.dtype)

    @pl.when(i == nc - 1)
    def _():
        sN_ref[0] = s_sc[...]


def gla_zero_state(b):
    return jnp.zeros((b, N_DIR, GLA_HEADS, GLA_DV, GLA_DK), F32)


def gla_scan(zb, zs, w2, b2, state, with_out=True):
    b, l, _ = zb.shape
    nc = l // CHUNK
    hk = GLA_HEADS * GLA_DK
    oq, ok, ov = OFF["d_q"] // hk, OFF["d_k"] // hk, OFF["d_v"] // W
    w2p = jnp.zeros((N_DIR, ZS_W, hk), F32)
    for d in range(N_DIR):
        w2p = w2p.at[d, N_IF + d * GLA_RANK:N_IF + (d + 1) * GLA_RANK, :].set(w2[d].astype(F32))
    b2r = b2.astype(F32).reshape(N_DIR, 1, hk)
    jm = jnp.asarray(np.eye(CHUNK, dtype=np.float32)[::-1].copy(), BF16)

    def _ch(i, dirn):
        return i if dirn == 0 else nc - 1 - i

    def blk(width, off, dirn):
        return pl.BlockSpec((1, CHUNK, width), lambda bi, i: (bi, _ch(i, dirn), off))

    def full(shape):
        return pl.BlockSpec(shape, lambda bi, i: (0,) * len(shape))

    s_shape = (N_DIR, GLA_HEADS, GLA_DV, GLA_DK)
    s_spec = pl.BlockSpec((1,) + s_shape, lambda bi, i: (bi, 0, 0, 0, 0))
    in_specs, args = [], []
    for dirn in range(N_DIR):
        in_specs += [blk(hk, oq, dirn), blk(hk, ok, dirn), blk(W, ov, dirn), blk(ZS_W, 0, dirn)]
        args += [zb, zb, zb, zs]
    in_specs += [full((N_DIR, ZS_W, hk)), full((N_DIR, 1, hk)), full((CHUNK, CHUNK)),
                 full((len(GLA_LEVELS) + 1, CHUNK, CHUNK)), s_spec]
    args += [w2p, b2r, jm, _gla_masks(), state]
    o_shape = jax.ShapeDtypeStruct((b, l, W), BF16)
    out_f, out_b, s_n = pl.pallas_call(
        functools.partial(_gla_body, nc, with_out),
        grid=(b, nc),
        in_specs=in_specs,
        out_specs=[blk(W, 0, 0), blk(W, 0, 1), s_spec],
        out_shape=[o_shape, o_shape, jax.ShapeDtypeStruct((b,) + s_shape, F32)],
        scratch_shapes=[pltpu.VMEM(s_shape, F32)],
        compiler_params=_cparams(2),
        name="gla_scan",
    )(*args)
    return out_f, out_b, s_n


def _matmul_body(a_ref, b_ref, o_ref):
    o_ref[...] = jnp.dot(a_ref[...], b_ref[...], preferred_element_type=F32).astype(o_ref.dtype)


def _pick_tile(n, cands):
    for c in cands:
        if n % c == 0:
            return c
    raise ValueError(f"no tile for {n}")


def pallas_matmul(a, b, out_dtype=F32):
    m, k = a.shape
    k2, n = b.shape
    assert k == k2
    n_pad = -(-n // 128) * 128
    if n_pad != n:
        b = jnp.pad(b, ((0, 0), (0, n_pad - n)))
    m_pad = -(-m // 16) * 16
    if m_pad != m:
        a = jnp.pad(a, ((0, m_pad - m), (0, 0)))
    tm = _pick_tile(m_pad, (1024, 512, 256, 128, 64, 32, 16))
    tn = _pick_tile(n_pad, (512, 256, 128))
    out = pl.pallas_call(
        _matmul_body,
        grid=(m_pad // tm, n_pad // tn),
        in_specs=[pl.BlockSpec((tm, k), lambda i, j: (i, 0)),
                  pl.BlockSpec((k, tn), lambda i, j: (0, j))],
        out_specs=pl.BlockSpec((tm, tn), lambda i, j: (i, j)),
        out_shape=jax.ShapeDtypeStruct((m_pad, n_pad), out_dtype),
        compiler_params=_cparams(2),
        name="proj_matmul",
    )(a.astype(BF16), b.astype(BF16))
    if m_pad != m or n_pad != n:
        out = out[:m, :n]
    return out


def mm(a, b, out_dtype=F32):
    lead = a.shape[:-1]
    out = pallas_matmul(a.reshape(-1, a.shape[-1]), b, out_dtype)
    return out.reshape(*lead, b.shape[-1])


def rmsnorm(x, g):
    xf = x.astype(F32)
    y = xf * lax.rsqrt(jnp.mean(xf * xf, axis=-1, keepdims=True) + EPS)
    return (y * g.astype(F32)).astype(x.dtype)


def repack_w_in(w_in):
    wbig = jnp.concatenate([w_in[:, :ORIG_IF], w_in[:, ORIG_IF + N_IF:ORIG_DA]], axis=1)
    wsmall = jnp.concatenate([w_in[:, ORIG_IF:ORIG_IF + N_IF], w_in[:, ORIG_DA:ORIG_DA + N_DA]], axis=1)
    wsmall = jnp.pad(wsmall, ((0, 0), (0, ZS_W - N_IF - N_DA)))
    return wbig.astype(BF16), wsmall.astype(BF16)


def short_conv(x, w, b):
    xp = jnp.pad(x, ((0, 0), (1, 1), (0, 0)))
    return xp[:, :-2] * w[0] + xp[:, 1:-1] * w[1] + xp[:, 2:] * w[2] + b


def fourier_mix(u, w):
    b, l, _ = u.shape
    uf = u.astype(F32).reshape(b, l, FNET_GROUPS, FNET_GW)
    y = jnp.fft.fftn(uf, axes=(1, 3), norm="ortho").real.reshape(b, l, GROUP_W)
    return mm(y, w)


def hyena_filter_spectrum(l, p):
    hid_dim = p["hy_f_w2"].shape[0]
    t = jnp.linspace(0.0, 1.0, l, dtype=F32)[:, None]
    bands = jnp.linspace(1e-4, HYENA_BANDS - 1, HYENA_BANDS, dtype=F32)[None, :]
    ang = (2.0 * math.pi / l) * jnp.arange(l, dtype=F32)[:, None] * bands
    feats = jnp.concatenate([t, jnp.cos(ang), -jnp.sin(ang)], axis=-1)
    fr = p["hy_f_freq"].astype(F32)
    hid = jnp.sin(fr[0] * (feats @ p["hy_f_w1"].astype(F32) + p["hy_f_b1"].astype(F32)))
    hid = jnp.sin(fr[1] * (hid @ p["hy_f_w2"].astype(F32) + p["hy_f_b2"].astype(F32)))
    filt = (hid @ p["hy_f_w3"].astype(F32)).reshape(l, HYENA_ORDER, N_DIR, GROUP_W)
    deltas = jnp.abs(jnp.linspace(HYENA_SLOW_DECAY, HYENA_FAST_DECAY, GROUP_W, dtype=F32))
    filt = filt * jnp.exp(-t * deltas)[:, None, None, :]
    fwd, bwd = filt[:, :, 0], filt[:, :, 1]
    k = jnp.concatenate([fwd, jnp.zeros_like(fwd[:1]), bwd[:0:-1]], axis=0)
    k = k / (jnp.sum(jnp.abs(k), axis=0, keepdims=True) + EPS)
    return jnp.fft.rfft(k, axis=0)


def long_conv(z, k_hat, bias):
    l = z.shape[1]
    y = jnp.fft.irfft(jnp.fft.rfft(z, n=2 * l, axis=1) * k_hat, n=2 * l, axis=1)[:, :l]
    return y + bias * z


def hyena_mix(u, conv_w, conv_b, k_hat, bias):
    u = short_conv(u, conv_w, conv_b).astype(F32)
    v, x1, x2 = jnp.split(u, 3, axis=-1)
    bias = bias.astype(F32)
    z = x1 * long_conv(v, k_hat[:, 0], bias[0])
    return x2 * long_conv(z, k_hat[:, 1], bias[1])


def gla_out(o, g):
    b, l, _ = o.shape
    oh = o.reshape(b, l, GLA_HEADS, GLA_DV)
    oh = oh * lax.rsqrt(jnp.mean(oh * oh, axis=-1, keepdims=True) + EPS)
    return oh.reshape(b, l, GROUP_W) * g.astype(F32)


def zcol(zb, name, width):
    return zb[..., OFF[name]:OFF[name] + width].astype(F32)


def combine(zb, o_m, o_g, p):
    l = zb.shape[1]

    def silu(a):
        return jax.nn.silu(a)
    y_a = fourier_mix(zcol(zb, "a_u", W), p["fnet_w"]) * silu(zcol(zb, "a_g", W))
    y_b = hyena_mix(zcol(zb, "b_u", 3 * W), p["hy_conv_w"], p["hy_conv_b"], hyena_filter_spectrum(l, p),
                    p["hy_bias"]) * silu(zcol(zb, "b_g", W))
    y_c = o_m * jax.nn.sigmoid(zcol(zb, "c_o", W)) * silu(zcol(zb, "c_g", W))
    y_d = gla_out(o_g, p["gla_norm_g"]) * silu(zcol(zb, "d_g", W))
    y = jnp.concatenate([y_a, y_b, y_c, y_d], axis=-1)
    return mm(y, p["w_out"])


def mixer_layer(x, xc, c, c_ctx, p, last):
    b, l, d = x.shape
    lc = xc.shape[1]
    mod = mm(jax.nn.silu(c), p["ada_w"]) + p["ada_b"]
    shift, scale, gate = jnp.split(mod, 3, axis=-1)
    h = rmsnorm(x, p["norm_g"]) * (1 + scale[:, None]) + shift[:, None]
    n_mod = 2 if last else 3
    mod_c = mm(jax.nn.silu(c_ctx)[None], p["ada_w"][:, :n_mod * d])[0] + p["ada_b"][:n_mod * d]
    hc = rmsnorm(xc, p["norm_g"]) * (1 + mod_c[d:2 * d]) + mod_c[:d]
    wbig, wsmall = repack_w_in(p["w_in"])
    zb = mm(h, wbig, BF16)
    zs = mm(h, wsmall)
    zcb = mm(hc, wbig, BF16)
    zcs = mm(hc, wsmall)

    cmf, cmb, m_states = mlstm_scan(zcb, zcs, p["ml_conv_w"], p["ml_conv_b"], p["ml_gate_b"],
                                    mlstm_zero_state(b), col_major=False)
    lmf, lmb, _ = mlstm_scan(zb, zs, p["ml_conv_w"], p["ml_conv_b"], p["ml_gate_b"], m_states, col_major=True)
    lat_m = lmf.astype(F32) + lmb.astype(F32)

    cgf, cgb, g_states = gla_scan(zcb, zcs, p["gla_w2"], p["gla_b2"], gla_zero_state(b))
    lgf, lgb, _ = gla_scan(zb, zs, p["gla_w2"], p["gla_b2"], g_states)
    lat_g = lgf.astype(F32) + lgb.astype(F32)

    x = x + gate[:, None] * combine(zb, lat_m, lat_g, p)
    if not last:
        ctx_m = cmf.astype(F32) + cmb.astype(F32)
        ctx_g = cgf.astype(F32) + cgb.astype(F32)
        xc = xc + mod_c[2 * d:] * combine(zcb, ctx_m, ctx_g, p)
    return x, xc


def kernel(x, c, ctx, c_ctx, ada_w, ada_b, norm_g, w_in, fnet_w, hy_conv_w, hy_conv_b, hy_f_w1, hy_f_b1,
           hy_f_w2, hy_f_b2, hy_f_w3, hy_f_freq, hy_bias, ml_conv_w, ml_conv_b, ml_gate_b, gla_w2, gla_b2,
           gla_norm_g, w_out, final_g):
    xc = ctx
    depth = ada_w.shape[0]
    for i in range(depth):
        p = {
            "ada_w": ada_w[i], "ada_b": ada_b[i], "norm_g": norm_g[i], "w_in": w_in[i], "fnet_w": fnet_w[i],
            "hy_conv_w": hy_conv_w[i], "hy_conv_b": hy_conv_b[i], "hy_f_w1": hy_f_w1[i], "hy_f_b1": hy_f_b1[i],
            "hy_f_w2": hy_f_w2[i], "hy_f_b2": hy_f_b2[i], "hy_f_w3": hy_f_w3[i], "hy_f_freq": hy_f_freq[i],
            "hy_bias": hy_bias[i], "ml_conv_w": ml_conv_w[i], "ml_conv_b": ml_conv_b[i], "ml_gate_b": ml_gate_b[i],
            "gla_w2": gla_w2[i], "gla_b2": gla_b2[i], "gla_norm_g": gla_norm_g[i], "w_out": w_out[i],
        }
        x, xc = mixer_layer(x, xc, c, c_ctx, p, i == depth - 1)
    return rmsnorm(x, final_g)
```

```python
import functools
import math

import numpy as np
import jax
import jax.numpy as jnp
from jax import lax
from jax.experimental import pallas as pl
from jax.experimental.pallas import tpu as pltpu

D_MODEL = 4096
GRID_W = 64
N_GROUPS = 4
GROUP_W = D_MODEL // N_GROUPS
FNET_GROUPS = 4
FNET_GW = GROUP_W // FNET_GROUPS
HYENA_ORDER = 2
HYENA_BANDS = 16
HYENA_FAST_DECAY = math.log(1e-2) / 0.3
HYENA_SLOW_DECAY = math.log(1e-2) / 1.5
MLSTM_HEADS = 4
MLSTM_DH = GROUP_W // MLSTM_HEADS
GLA_HEADS = 4
GLA_DV = GROUP_W // GLA_HEADS
GLA_DK = GLA_DV // 2
GLA_RANK = 16
GLA_TAU = 16.0
N_DIR = 2
CHUNK = 64
EPS = 1e-6
F32 = jnp.float32
BF16 = jnp.bfloat16

W = GROUP_W
OFF = {"a_u": 0, "a_g": W, "b_u": 2 * W, "b_g": 5 * W, "c_q": 6 * W, "c_k": 7 * W, "c_v": 8 * W,
       "c_o": 9 * W, "c_g": 10 * W, "d_q": 11 * W, "d_k": 11 * W + 512, "d_v": 12 * W, "d_g": 13 * W}
ZB_W = 14 * W
ZS_W = 128
N_IF = N_DIR * 2 * MLSTM_HEADS
N_DA = N_DIR * GLA_RANK
ORIG_IF = 11 * W
ORIG_DA = ORIG_IF + N_IF + 2 * 512 + 2 * W

V7X_VMEM_LIMIT_BYTES = 56 * 1024 * 1024
SUBLANES = 8


def _cparams(n_axes):
    return pltpu.CompilerParams(dimension_semantics=("arbitrary",) * n_axes,
                                vmem_limit_bytes=V7X_VMEM_LIMIT_BYTES)


def _split3(a):
    hi = a.astype(BF16)
    r1 = a - hi.astype(F32)
    mid = r1.astype(BF16)
    lo = (r1 - mid.astype(F32)).astype(BF16)
    return hi, mid, lo


def _dot(a, b):
    return jnp.dot(a, b, preferred_element_type=F32)


def _dot_nt(a, b):
    return lax.dot_general(a, b, (((1,), (1,)), ((), ())), preferred_element_type=F32)


def _dot_tn(a, b):
    return lax.dot_general(a, b, (((0,), (0,)), ((), ())), preferred_element_type=F32)


def _dot_exact_lhs(m_bf16, x):
    hi, mid, lo = _split3(x)
    return _dot(m_bf16, hi) + _dot(m_bf16, mid) + _dot(m_bf16, lo)


def _dot_exact_rhs(x, m_bf16):
    hi, mid, lo = _split3(x)
    return _dot(hi, m_bf16) + _dot(mid, m_bf16) + _dot(lo, m_bf16)


def _dot_f32x3(a, b):
    ah = a.astype(BF16)
    al = (a - ah.astype(F32)).astype(BF16)
    bh = b.astype(BF16)
    bl = (b - bh.astype(F32)).astype(BF16)
    return _dot(ah, bh) + _dot(ah, bl) + _dot(al, bh)


def _log_sigmoid(x):
    return jnp.minimum(x, 0.0) - jnp.log1p(jnp.exp(-jnp.abs(x)))


def _silu(x):
    return x * jax.nn.sigmoid(x)


def _row_iota(shape):
    return lax.broadcasted_iota(jnp.int32, shape, 0)


def _shift_down(x, d):
    n = x.shape[0]
    if d % SUBLANES == 0:
        return jnp.concatenate([jnp.zeros((d,) + x.shape[1:], x.dtype), x[: n - d]], axis=0)
    return jnp.where(_row_iota(x.shape) >= d, pltpu.roll(x, d, 0), 0.0)


def _shift_up(x, d):
    n = x.shape[0]
    if d == 0:
        return x
    if d % SUBLANES == 0:
        return jnp.concatenate([x[d:], jnp.zeros((d,) + x.shape[1:], x.dtype)], axis=0)
    return jnp.where(_row_iota(x.shape) < n - d, pltpu.roll(x, n - d, 0), 0.0)


def _conv3_silu(x_ref, prev_ref, next_ref, w_ref, tap0, bias_row, is_first, is_last):
    x = x_ref[0].astype(F32)
    n = x.shape[0]
    rows = _row_iota(x.shape)
    prev_row = jnp.where(is_first, 0.0, prev_ref[0][SUBLANES - 1:SUBLANES, :].astype(F32))
    next_row = jnp.where(is_last, 0.0, next_ref[0][0:1, :].astype(F32))
    xm = jnp.where(rows == 0, prev_row, pltpu.roll(x, 1, 0))
    xp = jnp.where(rows == n - 1, next_row, pltpu.roll(x, n - 1, 0))
    y = (xm * w_ref[tap0:tap0 + 1, :] + x * w_ref[tap0 + 1:tap0 + 2, :] + xp * w_ref[tap0 + 2:tap0 + 3, :]
         + w_ref[bias_row:bias_row + 1, :])
    return _silu(y)


def _mlstm_body(nc, *refs):
    (qf, kf, vf, qfp, qfn, kfp, kfn, gcf, grf,
     qb, kb, vb, qbp, qbn, kbp, kbn, gcb, grb,
     cw_ref, gbc_ref, gbr_ref, tri_ref, c0_ref, n0_ref, m0_ref,
     of_ref, ob_ref, cN_ref, nN_ref, mN_ref, c_sc, n_sc, m_sc) = refs
    i = pl.program_id(1)

    @pl.when(i == 0)
    def _():
        c_sc[...] = c0_ref[0]
        n_sc[...] = n0_ref[0]
        m_sc[...] = m0_ref[0]

    dh = MLSTM_DH
    for d in range(N_DIR):
        q_ref, k_ref, v_ref, qp, qn, kp, kn, gc_ref, gr_ref, o_ref = (
            (qf, kf, vf, qfp, qfn, kfp, kfn, gcf, grf, of_ref) if d == 0 else
            (qb, kb, vb, qbp, qbn, kbp, kbn, gcb, grb, ob_ref))
        chunk = i if d == 0 else nc - 1 - i
        is_first, is_last = chunk == 0, chunk == nc - 1
        q = _conv3_silu(q_ref, qp, qn, cw_ref, 0, 6, is_first, is_last)
        k = _conv3_silu(k_ref, kp, kn, cw_ref, 3, 7, is_first, is_last) * (dh ** -0.5)
        v = v_ref[0]
        tri = tri_ref[d]
        tri_t = tri_ref[1 - d]
        gcol = gc_ref[0] + gbc_ref[...]
        grow = gr_ref[0, 0] + gbr_ref[...]
        bcum_c = _dot_exact_lhs(tri, _log_sigmoid(gcol))
        bcum_r = _dot_exact_rhs(_log_sigmoid(grow), tri_t)
        mask = tri.astype(F32) > 0.5
        last = CHUNK - 1 if d == 0 else 0
        q16, k16 = q.astype(BF16), k.astype(BF16)
        outs = []
        for h in range(MLSTM_HEADS):
            ci, cf = d * 8 + h, d * 8 + MLSTM_HEADS + h
            bc = bcum_c[:, cf:cf + 1]
            li_c = gcol[:, ci:ci + 1]
            br = bcum_r[cf:cf + 1, :]
            li_r = grow[ci:ci + 1, :]
            m_st = m_sc[d, h][0:1, 0:1]
            logw = jnp.where(mask, bc - (br - li_r), -jnp.inf)
            m_pos = jnp.maximum(bc + m_st, jnp.max(logw, axis=1, keepdims=True))
            w_intra = jnp.exp(logw - m_pos)
            w_prev = jnp.exp(bc + m_st - m_pos)
            sl = slice(h * dh, (h + 1) * dh)
            qh, kh, vh = q16[:, sl], k16[:, sl], v[:, sl]
            c_st = c_sc[d, h]
            n_st = n_sc[d, h][0:1, :]
            s = _dot_nt(qh, kh) * w_intra
            num = _dot(s.astype(BF16), vh) + w_prev * _dot(qh, c_st.astype(BF16))
            den = jnp.sum(s, axis=1, keepdims=True) + w_prev * jnp.sum(q[:, sl] * n_st, axis=1, keepdims=True)
            outs.append(num / jnp.maximum(jnp.abs(den), jnp.exp(-m_pos)))
            m_next = m_pos[last:last + 1, :]
            bc_last = bc[last:last + 1, :]
            w_state = jnp.exp(bc_last - bc + li_c - m_next)
            decay = jnp.exp(bc_last + m_st - m_next)
            kw = k[:, sl] * w_state
            c_sc[d, h] = decay * c_st + _dot_tn(kw.astype(BF16), vh)
            n_sc[d, h] = jnp.broadcast_to(decay * n_st + jnp.sum(kw, axis=0, keepdims=True), (SUBLANES, dh))
            m_sc[d, h] = jnp.broadcast_to(m_next, (SUBLANES, 128))
        o_ref[0] = jnp.concatenate(outs, axis=1).astype(o_ref.dtype)

    @pl.when(i == nc - 1)
    def _():
        cN_ref[0] = c_sc[...]
        nN_ref[0] = n_sc[...]
        mN_ref[0] = m_sc[...]


def _tri_consts():
    t = np.tril(np.ones((CHUNK, CHUNK), np.float32))
    return jnp.asarray(np.stack([t, t.T]), BF16)


def mlstm_zero_state(b):
    return (jnp.zeros((b, N_DIR, MLSTM_HEADS, MLSTM_DH, MLSTM_DH), F32),
            jnp.zeros((b, N_DIR, MLSTM_HEADS, SUBLANES, MLSTM_DH), F32),
            jnp.zeros((b, N_DIR, MLSTM_HEADS, SUBLANES, 128), F32))


def mlstm_scan(zb, zs, conv_w, conv_b, gate_b, state, col_major):
    b, l, _ = zb.shape
    nc = l // CHUNK
    nblk = ZB_W // W
    oq, ok, ov = OFF["c_q"] // W, OFF["c_k"] // W, OFF["c_v"] // W
    cw8 = jnp.concatenate([conv_w.reshape(6, W), conv_b.reshape(2, W)], axis=0).astype(F32)
    gb = gate_b.astype(F32).reshape(N_IF)
    gbc = jnp.zeros((1, ZS_W), F32).at[0, :N_IF].set(gb)
    gbr = jnp.broadcast_to(gb[:, None], (N_IF, CHUNK))
    if col_major:
        rows = l // GRID_W
        assert rows == CHUNK and GRID_W == nc
        zb_v = zb.reshape(b, rows, GRID_W * ZB_W)
        zs_v = zs.reshape(b, rows, GRID_W * ZS_W)
        grow = zs.reshape(b, rows, GRID_W, ZS_W)[..., :N_IF].transpose(0, 2, 3, 1)
        last_rb = rows // SUBLANES - 1

        def main(off, dirn):
            return pl.BlockSpec((1, CHUNK, W), lambda bi, i: (bi, 0, _ch(i, dirn) * nblk + off))

        def prev(off, dirn):
            return pl.BlockSpec((1, SUBLANES, W),
                                lambda bi, i: (bi, last_rb, jnp.maximum(_ch(i, dirn) - 1, 0) * nblk + off))

        def nxt(off, dirn):
            return pl.BlockSpec((1, SUBLANES, W),
                                lambda bi, i: (bi, 0, jnp.minimum(_ch(i, dirn) + 1, nc - 1) * nblk + off))

        def gcol(dirn):
            return pl.BlockSpec((1, CHUNK, ZS_W), lambda bi, i: (bi, 0, _ch(i, dirn)))

        def out_spec(dirn):
            return pl.BlockSpec((1, CHUNK, W), lambda bi, i: (bi, 0, _ch(i, dirn)))
        out_shape_o = jax.ShapeDtypeStruct((b, rows, GRID_W * W), BF16)
    else:
        zb_v, zs_v = zb, zs
        grow = zs.reshape(b, nc, CHUNK, ZS_W)[..., :N_IF].transpose(0, 1, 3, 2)
        cpb = CHUNK // SUBLANES

        def main(off, dirn):
            return pl.BlockSpec((1, CHUNK, W), lambda bi, i: (bi, _ch(i, dirn), off))

        def prev(off, dirn):
            return pl.BlockSpec((1, SUBLANES, W),
                                lambda bi, i: (bi, jnp.maximum(_ch(i, dirn) * cpb - 1, 0), off))

        def nxt(off, dirn):
            return pl.BlockSpec((1, SUBLANES, W),
                                lambda bi, i: (bi, jnp.minimum((_ch(i, dirn) + 1) * cpb, nc * cpb - 1), off))

        def gcol(dirn):
            return pl.BlockSpec((1, CHUNK, ZS_W), lambda bi, i: (bi, _ch(i, dirn), 0))

        def out_spec(dirn):
            return pl.BlockSpec((1, CHUNK, W), lambda bi, i: (bi, _ch(i, dirn), 0))
        out_shape_o = jax.ShapeDtypeStruct((b, l, W), BF16)

    def _ch(i, dirn):
        return i if dirn == 0 else nc - 1 - i

    def grow_spec(dirn):
        return pl.BlockSpec((1, 1, N_IF, CHUNK), lambda bi, i: (bi, _ch(i, dirn), 0, 0))

    def full(shape):
        return pl.BlockSpec(shape, lambda bi, i: (0,) * len(shape))

    def per_b(shape):
        return pl.BlockSpec((1,) + shape, lambda bi, i: (bi,) + (0,) * len(shape))

    in_specs, args = [], []
    for dirn in range(N_DIR):
        in_specs += [main(oq, dirn), main(ok, dirn), main(ov, dirn), prev(oq, dirn), nxt(oq, dirn),
                     prev(ok, dirn), nxt(ok, dirn), gcol(dirn), grow_spec(dirn)]
        args += [zb_v, zb_v, zb_v, zb_v, zb_v, zb_v, zb_v, zs_v, grow]
    c_shape = (N_DIR, MLSTM_HEADS, MLSTM_DH, MLSTM_DH)
    n_shape = (N_DIR, MLSTM_HEADS, SUBLANES, MLSTM_DH)
    m_shape = (N_DIR, MLSTM_HEADS, SUBLANES, 128)
    in_specs += [full((8, W)), full((1, ZS_W)), full((N_IF, CHUNK)), full((2, CHUNK, CHUNK)),
                 per_b(c_shape), per_b(n_shape), per_b(m_shape)]
    args += [cw8, gbc, gbr, _tri_consts(), *state]
    out_f, out_b, c_n, n_n, m_n = pl.pallas_call(
        functools.partial(_mlstm_body, nc),
        grid=(b, nc),
        in_specs=in_specs,
        out_specs=[out_spec(0), out_spec(1), per_b(c_shape), per_b(n_shape), per_b(m_shape)],
        out_shape=[out_shape_o, out_shape_o,
                   jax.ShapeDtypeStruct((b,) + c_shape, F32), jax.ShapeDtypeStruct((b,) + n_shape, F32),
                   jax.ShapeDtypeStruct((b,) + m_shape, F32)],
        scratch_shapes=[pltpu.VMEM(c_shape, F32), pltpu.VMEM(n_shape, F32), pltpu.VMEM(m_shape, F32)],
        compiler_params=_cparams(2),
        name="mlstm_scan",
    )(*args)
    return out_f.reshape(b, l, W), out_b.reshape(b, l, W), (c_n, n_n, m_n)


GLA_LEVELS = (1, 2, 4, 8, 16, 32)


def _gla_masks():
    t = np.arange(CHUNK)[:, None]
    s = np.arange(CHUNK)[None, :]
    ms = [(t == s)]
    for m in GLA_LEVELS:
        ms.append((t // (2 * m) == s // (2 * m)) & (t % (2 * m) >= m) & (s % (2 * m) < m))
    return jnp.asarray(np.stack(ms).astype(np.float32))


def _cumsum_rows(x):
    d = 1
    while d < x.shape[0]:
        x = x + _shift_down(x, d)
        d *= 2
    return x


def _prev_block_end(x, m):
    n, w = x.shape
    if m >= SUBLANES:
        ends = x.reshape(n // m, m, w)[:, m - 1:m, :]
        prev = jnp.concatenate([jnp.zeros((1, 1, w), x.dtype), ends[:-1]], axis=0)
        return jnp.broadcast_to(prev, (n // m, m, w)).reshape(n, w)
    r = _row_iota(x.shape) % m
    y = _shift_down(x, 1)
    for j in range(1, m):
        y = jnp.where(r == j, _shift_down(x, j + 1), y)
    return y


def _own_block_end(x, m):
    n, w = x.shape
    if m >= SUBLANES:
        ends = x.reshape(n // m, m, w)[:, m - 1:m, :]
        return jnp.broadcast_to(ends, (n // m, m, w)).reshape(n, w)
    r = _row_iota(x.shape) % m
    y = x
    for j in range(m - 1):
        y = jnp.where(r == j, _shift_up(x, m - 1 - j), y)
    return y


def _gla_body(nc, with_out, *refs):
    (qf, kf, vf, af, qb, kb, vb, ab, w2_ref, b2_ref, j_ref, mask_ref, s0_ref,
     of_ref, ob_ref, sN_ref, s_sc) = refs
    i = pl.program_id(1)

    @pl.when(i == 0)
    def _():
        s_sc[...] = s0_ref[0]

    dk, dv = GLA_DK, GLA_DV
    for d in range(N_DIR):
        q_ref, k_ref, v_ref, a_ref, o_ref = (qf, kf, vf, af, of_ref) if d == 0 else (qb, kb, vb, ab, ob_ref)
        q16, k16, v16, a = q_ref[0], k_ref[0], v_ref[0], a_ref[0]
        if d == 1:
            jm = j_ref[...]
            q16 = _dot(jm, q16).astype(BF16)
            k16 = _dot(jm, k16).astype(BF16)
            v16 = _dot(jm, v16).astype(BF16)
            a = _dot_exact_lhs(jm, a)
        la = _log_sigmoid(_dot_f32x3(a, w2_ref[d]) + b2_ref[d]) * (1.0 / GLA_TAU)
        bc = _cumsum_rows(la)
        q = q16.astype(F32) * (dk ** -0.5)
        k = k16.astype(F32)
        b_last = bc[CHUNK - 1:CHUNK, :]
        q_in = (q * jnp.exp(bc)).astype(BF16)
        k_dec = (k * jnp.exp(b_last - bc)).astype(BF16)
        e_last = jnp.exp(b_last)
        if with_out:
            qs, ks = [q.astype(BF16)], [k16]
            for m in GLA_LEVELS:
                qs.append((q * jnp.exp(bc - _prev_block_end(bc, m))).astype(BF16))
                ks.append((k * jnp.exp(_own_block_end(bc, m) - bc)).astype(BF16))
        outs = []
        for h in range(GLA_HEADS):
            sk = slice(h * dk, (h + 1) * dk)
            sv = slice(h * dv, (h + 1) * dv)
            st = s_sc[d, h]
            if with_out:
                att = jnp.zeros((CHUNK, CHUNK), F32)
                for lvl in range(len(GLA_LEVELS) + 1):
                    att = att + mask_ref[lvl] * _dot_nt(qs[lvl][:, sk], ks[lvl][:, sk])
                outs.append(_dot(att.astype(BF16), v16[:, sv]) + _dot_nt(q_in[:, sk], st.astype(BF16)))
            s_sc[d, h] = st * e_last[:, sk] + _dot_tn(v16[:, sv], k_dec[:, sk])
        if with_out:
            o = jnp.concatenate(outs, axis=1).astype(BF16)
            if d == 1:
                o = _dot(j_ref[...], o).astype(BF16)
            o_ref[0] = o
        else:
            o_ref[0] = jnp.zeros(o_ref.shape[1:], o_ref.dtype)

    @pl.when(i == nc - 1)
    def _():
        sN_ref[0] = s_sc[...]


def gla_zero_state(b):
    return jnp.zeros((b, N_DIR, GLA_HEADS, GLA_DV, GLA_DK), F32)


def gla_scan(zb, zs, w2, b2, state, with_out=True):
    b, l, _ = zb.shape
    nc = l // CHUNK
    hk = GLA_HEADS * GLA_DK
    oq, ok, ov = OFF["d_q"] // hk, OFF["d_k"] // hk, OFF["d_v"] // W
    w2p = jnp.zeros((N_DIR, ZS_W, hk), F32)
    for d in range(N_DIR):
        w2p = w2p.at[d, N_IF + d * GLA_RANK:N_IF + (d + 1) * GLA_RANK, :].set(w2[d].astype(F32))
    b2r = b2.astype(F32).reshape(N_DIR, 1, hk)
    jm = jnp.asarray(np.eye(CHUNK, dtype=np.float32)[::-1].copy(), BF16)

    def _ch(i, dirn):
        return i if dirn == 0 else nc - 1 - i

    def blk(width, off, dirn):
        return pl.BlockSpec((1, CHUNK, width), lambda bi, i: (bi, _ch(i, dirn), off))

    def full(shape):
        return pl.BlockSpec(shape, lambda bi, i: (0,) * len(shape))

    s_shape = (N_DIR, GLA_HEADS, GLA_DV, GLA_DK)
    s_spec = pl.BlockSpec((1,) + s_shape, lambda bi, i: (bi, 0, 0, 0, 0))
    in_specs, args = [], []
    for dirn in range(N_DIR):
        in_specs += [blk(hk, oq, dirn), blk(hk, ok, dirn), blk(W, ov, dirn), blk(ZS_W, 0, dirn)]
        args += [zb, zb, zb, zs]
    in_specs += [full((N_DIR, ZS_W, hk)), full((N_DIR, 1, hk)), full((CHUNK, CHUNK)),
                 full((len(GLA_LEVELS) + 1, CHUNK, CHUNK)), s_spec]
    args += [w2p, b2r, jm, _gla_masks(), state]
    o_shape = jax.ShapeDtypeStruct((b, l, W), BF16)
    out_f, out_b, s_n = pl.pallas_call(
        functools.partial(_gla_body, nc, with_out),
        grid=(b, nc),
        in_specs=in_specs,
        out_specs=[blk(W, 0, 0), blk(W, 0, 1), s_spec],
        out_shape=[o_shape, o_shape, jax.ShapeDtypeStruct((b,) + s_shape, F32)],
        scratch_shapes=[pltpu.VMEM(s_shape, F32)],
        compiler_params=_cparams(2),
        name="gla_scan",
    )(*args)
    return out_f, out_b, s_n


def _matmul_body(a_ref, b_ref, o_ref):
    o_ref[...] = jnp.dot(a_ref[...], b_ref[...], preferred_element_type=F32).astype(o_ref.dtype)


def _pick_tile(n, cands):
    for c in cands:
        if n % c == 0:
            return c
    raise ValueError(f"no tile for {n}")


def pallas_matmul(a, b, out_dtype=F32):
    m, k = a.shape
    k2, n = b.shape
    assert k == k2
    n_pad = -(-n // 128) * 128
    if n_pad != n:
        b = jnp.pad(b, ((0, 0), (0, n_pad - n)))
    m_pad = -(-m // 16) * 16
    if m_pad != m:
        a = jnp.pad(a, ((0, m_pad - m), (0, 0)))
    tm_cands = (1024, 512, 256, 128, 64, 32, 16) if k <= 4096 else (512, 256, 128, 64, 32, 16)
    tm = _pick_tile(m_pad, tm_cands)
    tn = _pick_tile(n_pad, (512, 256, 128))
    out = pl.pallas_call(
        _matmul_body,
        grid=(m_pad // tm, n_pad // tn),
        in_specs=[pl.BlockSpec((tm, k), lambda i, j: (i, 0)),
                  pl.BlockSpec((k, tn), lambda i, j: (0, j))],
        out_specs=pl.BlockSpec((tm, tn), lambda i, j: (i, j)),
        out_shape=jax.ShapeDtypeStruct((m_pad, n_pad), out_dtype),
        compiler_params=_cparams(2),
        name="proj_matmul",
    )(a.astype(BF16), b.astype(BF16))
    if m_pad != m or n_pad != n:
        out = out[:m, :n]
    return out


def mm(a, b, out_dtype=F32):
    lead = a.shape[:-1]
    out = pallas_matmul(a.reshape(-1, a.shape[-1]), b, out_dtype)
    return out.reshape(*lead, b.shape[-1])


def rmsnorm(x, g):
    xf = x.astype(F32)
    y = xf * lax.rsqrt(jnp.mean(xf * xf, axis=-1, keepdims=True) + EPS)
    return (y * g.astype(F32)).astype(x.dtype)


def repack_w_in(w_in):
    wbig = jnp.concatenate([w_in[:, :ORIG_IF], w_in[:, ORIG_IF + N_IF:ORIG_DA]], axis=1)
    wsmall = jnp.concatenate([w_in[:, ORIG_IF:ORIG_IF + N_IF], w_in[:, ORIG_DA:ORIG_DA + N_DA]], axis=1)
    wsmall = jnp.pad(wsmall, ((0, 0), (0, ZS_W - N_IF - N_DA)))
    return wbig.astype(BF16), wsmall.astype(BF16)


def short_conv(x, w, b):
    xp = jnp.pad(x, ((0, 0), (1, 1), (0, 0)))
    return xp[:, :-2] * w[0] + xp[:, 1:-1] * w[1] + xp[:, 2:] * w[2] + b


DFT_BLK = 64


def dft_cos_sin(n_rows, n_cols, modulus):
    s = jnp.arange(n_cols, dtype=jnp.int32)[None, :]
    f0 = jnp.arange(DFT_BLK, dtype=jnp.int32)[:, None]
    f1 = jnp.arange(n_rows // DFT_BLK, dtype=jnp.int32)[:, None] * DFT_BLK

    def tab(f):
        ang = ((f * s) % modulus).astype(F32) * (2.0 * math.pi / modulus)
        return jnp.cos(ang), jnp.sin(ang)
    ca, sa = tab(f1)
    cb, sb = tab(f0)
    c = ca[:, None, :] * cb[None] - sa[:, None, :] * sb[None]
    sn = sa[:, None, :] * cb[None] + ca[:, None, :] * sb[None]
    return c.reshape(n_rows, n_cols), sn.reshape(n_rows, n_cols)


def fourier_mats(l):
    c, sn = dft_cos_sin(l, l, l)
    gf = jnp.concatenate([c, -sn], axis=1) * ((l * FNET_GW) ** -0.5)
    cg, sg = dft_cos_sin(FNET_GW, FNET_GW, FNET_GW)
    eye = jnp.eye(FNET_GROUPS, dtype=F32)
    bd = jnp.concatenate([jnp.kron(eye, cg), jnp.kron(eye, sg)], axis=1)
    return gf.astype(BF16), bd.astype(BF16)


def fourier_mix(u, w, mats):
    gf, bd = mats
    b, l, _ = u.shape
    p = mm(u, bd)
    pq = p.reshape(b, l, 2, W).transpose(2, 1, 0, 3).reshape(2 * l, b * W)
    y = pallas_matmul(gf, pq).reshape(l, b, W).transpose(1, 0, 2)
    return mm(y, w)


def hyena_mats(l):
    c, sn = dft_cos_sin(l, l, 2 * l)
    alt = jnp.where(jnp.arange(l) % 2 == 0, 1.0, -1.0).astype(F32)
    g = jnp.concatenate([c, sn.at[0].set(alt)], axis=0)
    return g.astype(BF16), g.T.astype(BF16), alt


def hyena_filter_coeffs(l, p, mats):
    g, _, alt = mats
    t = jnp.linspace(0.0, 1.0, l, dtype=F32)[:, None]
    bands = jnp.linspace(1e-4, HYENA_BANDS - 1, HYENA_BANDS, dtype=F32)[None, :]
    ang = (2.0 * math.pi / l) * jnp.arange(l, dtype=F32)[:, None] * bands
    feats = jnp.concatenate([t, jnp.cos(ang), -jnp.sin(ang)], axis=-1)
    fr = p["hy_f_freq"].astype(F32)
    hid = jnp.sin(fr[0] * (feats @ p["hy_f_w1"].astype(F32) + p["hy_f_b1"].astype(F32)))
    hid = jnp.sin(fr[1] * (hid @ p["hy_f_w2"].astype(F32) + p["hy_f_b2"].astype(F32)))
    filt = (hid @ p["hy_f_w3"].astype(F32)).reshape(l, HYENA_ORDER, N_DIR, GROUP_W)
    deltas = jnp.abs(jnp.linspace(HYENA_SLOW_DECAY, HYENA_FAST_DECAY, GROUP_W, dtype=F32))
    filt = filt * jnp.exp(-t * deltas)[:, None, None, :]
    fwd, bwd = filt[:, :, 0], filt[:, :, 1]
    bwd = bwd.at[0].set(0.0)
    norm = jnp.sum(jnp.abs(fwd), axis=0, keepdims=True) + jnp.sum(jnp.abs(bwd), axis=0, keepdims=True) + EPS
    n_col = HYENA_ORDER * GROUP_W
    fp = ((fwd + bwd) / norm).reshape(l, n_col)
    fm = ((fwd - bwd) / norm).reshape(l, n_col)
    kre = pallas_matmul(g[:l], fp)
    kim = -pallas_matmul(g[l:], fm)
    k_nyq = jnp.sum(alt[:, None] * fp, axis=0)
    row0 = (jnp.arange(l) == 0)[:, None]
    kim = jnp.where(row0, 0.0, kim)
    a4 = jnp.where(row0, k_nyq[None, :], kre)
    coef = jnp.where(row0, 0.5 / l, 1.0 / l)
    return tuple((coef * a).reshape(l, HYENA_ORDER, GROUP_W) for a in (kre, kim, a4))


def long_conv(zt, mats, ck):
    g, gt, _ = mats
    l, n = zt.shape
    c = ck[0].shape[-1]
    zhat = pallas_matmul(g, zt)
    zc = zhat[:l].reshape(l, n // c, c)
    zs = zhat[l:].reshape(l, n // c, c)
    kre, kim, a4 = (a[:, None, :] for a in ck)
    y = jnp.concatenate([zc * kre + zs * kim, zs * a4 - zc * kim], axis=0).astype(BF16)
    return pallas_matmul(gt, y.reshape(2 * l, n))


def hyena_mix(u, conv_w, conv_b, coeffs, bias, mats):
    b, l, _ = u.shape
    u = short_conv(u, conv_w, conv_b).astype(F32)

    def to_t(a):
        return a.transpose(1, 0, 2).reshape(l, b * GROUP_W)
    v, x1, x2 = (to_t(a) for a in jnp.split(u, 3, axis=-1))
    bias = jnp.tile(bias.astype(F32), (1, b))
    z = x1 * (long_conv(v, mats, tuple(a[:, 0] for a in coeffs)) + bias[0] * v)
    out = x2 * (long_conv(z, mats, tuple(a[:, 1] for a in coeffs)) + bias[1] * z)
    return out.reshape(l, b, GROUP_W).transpose(1, 0, 2)


def gla_out(o, g):
    b, l, _ = o.shape
    oh = o.reshape(b, l, GLA_HEADS, GLA_DV)
    oh = oh * lax.rsqrt(jnp.mean(oh * oh, axis=-1, keepdims=True) + EPS)
    return oh.reshape(b, l, GROUP_W) * g.astype(F32)


def zcol(zb, name, width):
    return zb[..., OFF[name]:OFF[name] + width].astype(F32)


def combine(zb, o_m, o_g, p, mats):
    l = zb.shape[1]
    f_mats, h_mats = mats

    def silu(a):
        return jax.nn.silu(a)
    y_a = fourier_mix(zcol(zb, "a_u", W), p["fnet_w"], f_mats) * silu(zcol(zb, "a_g", W))
    y_b = hyena_mix(zcol(zb, "b_u", 3 * W), p["hy_conv_w"], p["hy_conv_b"], hyena_filter_coeffs(l, p, h_mats),
                    p["hy_bias"], h_mats) * silu(zcol(zb, "b_g", W))
    y_c = o_m * jax.nn.sigmoid(zcol(zb, "c_o", W)) * silu(zcol(zb, "c_g", W))
    y_d = gla_out(o_g, p["gla_norm_g"]) * silu(zcol(zb, "d_g", W))
    y = jnp.concatenate([y_a, y_b, y_c, y_d], axis=-1)
    return mm(y, p["w_out"])


def mixer_layer(x, xc, c, c_ctx, p, last, mats, mats_c):
    b, l, d = x.shape
    mod = mm(jax.nn.silu(c), p["ada_w"]) + p["ada_b"]
    shift, scale, gate = jnp.split(mod, 3, axis=-1)
    h = rmsnorm(x, p["norm_g"]) * (1 + scale[:, None]) + shift[:, None]
    n_mod = 2 if last else 3
    mod_c = mm(jax.nn.silu(c_ctx)[None], p["ada_w"][:, :n_mod * d])[0] + p["ada_b"][:n_mod * d]
    hc = rmsnorm(xc, p["norm_g"]) * (1 + mod_c[d:2 * d]) + mod_c[:d]
    wbig, wsmall = repack_w_in(p["w_in"])
    zb = mm(h, wbig, BF16)
    zs = mm(h, wsmall)
    zcb = mm(hc, wbig, BF16)
    zcs = mm(hc, wsmall)

    cmf, cmb, m_states = mlstm_scan(zcb, zcs, p["ml_conv_w"], p["ml_conv_b"], p["ml_gate_b"],
                                    mlstm_zero_state(b), col_major=False)
    lmf, lmb, _ = mlstm_scan(zb, zs, p["ml_conv_w"], p["ml_conv_b"], p["ml_gate_b"], m_states, col_major=True)
    lat_m = lmf.astype(F32) + lmb.astype(F32)

    cgf, cgb, g_states = gla_scan(zcb, zcs, p["gla_w2"], p["gla_b2"], gla_zero_state(b))
    lgf, lgb, _ = gla_scan(zb, zs, p["gla_w2"], p["gla_b2"], g_states)
    lat_g = lgf.astype(F32) + lgb.astype(F32)

    x = x + gate[:, None] * combine(zb, lat_m, lat_g, p, mats)
    if not last:
        ctx_m = cmf.astype(F32) + cmb.astype(F32)
        ctx_g = cgf.astype(F32) + cgb.astype(F32)
        xc = xc + mod_c[2 * d:] * combine(zcb, ctx_m, ctx_g, p, mats_c)
    return x, xc


def kernel(x, c, ctx, c_ctx, ada_w, ada_b, norm_g, w_in, fnet_w, hy_conv_w, hy_conv_b, hy_f_w1, hy_f_b1,
           hy_f_w2, hy_f_b2, hy_f_w3, hy_f_freq, hy_bias, ml_conv_w, ml_conv_b, ml_gate_b, gla_w2, gla_b2,
           gla_norm_g, w_out, final_g):
    xc = ctx
    depth = ada_w.shape[0]
    mats = (fourier_mats(x.shape[1]), hyena_mats(x.shape[1]))
    mats_c = (fourier_mats(ctx.shape[1]), hyena_mats(ctx.shape[1]))
    for i in range(depth):
        p = {
            "ada_w": ada_w[i], "ada_b": ada_b[i], "norm_g": norm_g[i], "w_in": w_in[i], "fnet_w": fnet_w[i],
            "hy_conv_w": hy_conv_w[i], "hy_conv_b": hy_conv_b[i], "hy_f_w1": hy_f_w1[i], "hy_f_b1": hy_f_b1[i],
            "hy_f_w2": hy_f_w2[i], "hy_f_b2": hy_f_b2[i], "hy_f_w3": hy_f_w3[i], "hy_f_freq": hy_f_freq[i],
            "hy_bias": hy_bias[i], "ml_conv_w": ml_conv_w[i], "ml_conv_b": ml_conv_b[i], "ml_gate_b": ml_gate_b[i],
            "gla_w2": gla_w2[i], "gla_b2": gla_b2[i], "gla_norm_g": gla_norm_g[i], "w_out": w_out[i],
        }
        x, xc = mixer_layer(x, xc, c, c_ctx, p, i == depth - 1, mats, mats_c)
    return rmsnorm(x, final_g)
```

```python
import functools
import math

import numpy as np
import jax
import jax.numpy as jnp
from jax import lax
from jax.experimental import pallas as pl
from jax.experimental.pallas import tpu as pltpu

D_MODEL = 4096
GRID_W = 64
N_GROUPS = 4
GROUP_W = D_MODEL // N_GROUPS
FNET_GROUPS = 4
FNET_GW = GROUP_W // FNET_GROUPS
HYENA_ORDER = 2
HYENA_BANDS = 16
HYENA_FAST_DECAY = math.log(1e-2) / 0.3
HYENA_SLOW_DECAY = math.log(1e-2) / 1.5
MLSTM_HEADS = 4
MLSTM_DH = GROUP_W // MLSTM_HEADS
GLA_HEADS = 4
GLA_DV = GROUP_W // GLA_HEADS
GLA_DK = GLA_DV // 2
GLA_RANK = 16
GLA_TAU = 16.0
N_DIR = 2
CHUNK = 64
EPS = 1e-6
F32 = jnp.float32
BF16 = jnp.bfloat16

W = GROUP_W
OFF = {"a_u": 0, "a_g": W, "b_u": 2 * W, "b_g": 5 * W, "c_q": 6 * W, "c_k": 7 * W, "c_v": 8 * W,
       "c_o": 9 * W, "c_g": 10 * W, "d_q": 11 * W, "d_k": 11 * W + 512, "d_v": 12 * W, "d_g": 13 * W}
ZB_W = 14 * W
ZS_W = 128
N_IF = N_DIR * 2 * MLSTM_HEADS
N_DA = N_DIR * GLA_RANK
ORIG_IF = 11 * W
ORIG_DA = ORIG_IF + N_IF + 2 * 512 + 2 * W

V7X_VMEM_LIMIT_BYTES = 56 * 1024 * 1024
SUBLANES = 8


def _cparams(n_axes):
    return pltpu.CompilerParams(dimension_semantics=("arbitrary",) * n_axes,
                                vmem_limit_bytes=V7X_VMEM_LIMIT_BYTES)


def _split3(a):
    hi = a.astype(BF16)
    r1 = a - hi.astype(F32)
    mid = r1.astype(BF16)
    lo = (r1 - mid.astype(F32)).astype(BF16)
    return hi, mid, lo


def _dot(a, b):
    return jnp.dot(a, b, preferred_element_type=F32)


def _dot_nt(a, b):
    return lax.dot_general(a, b, (((1,), (1,)), ((), ())), preferred_element_type=F32)


def _dot_tn(a, b):
    return lax.dot_general(a, b, (((0,), (0,)), ((), ())), preferred_element_type=F32)


def _dot_exact_lhs(m_bf16, x):
    hi, mid, lo = _split3(x)
    return _dot(m_bf16, hi) + _dot(m_bf16, mid) + _dot(m_bf16, lo)


def _dot_exact_rhs(x, m_bf16):
    hi, mid, lo = _split3(x)
    return _dot(hi, m_bf16) + _dot(mid, m_bf16) + _dot(lo, m_bf16)


def _dot_f32x3(a, b):
    ah = a.astype(BF16)
    al = (a - ah.astype(F32)).astype(BF16)
    bh = b.astype(BF16)
    bl = (b - bh.astype(F32)).astype(BF16)
    return _dot(ah, bh) + _dot(ah, bl) + _dot(al, bh)


def _log_sigmoid(x):
    return jnp.minimum(x, 0.0) - jnp.log1p(jnp.exp(-jnp.abs(x)))


def _silu(x):
    return x * jax.nn.sigmoid(x)


def _row_iota(shape):
    return lax.broadcasted_iota(jnp.int32, shape, 0)


def _shift_down(x, d):
    n = x.shape[0]
    if d % SUBLANES == 0:
        return jnp.concatenate([jnp.zeros((d,) + x.shape[1:], x.dtype), x[: n - d]], axis=0)
    return jnp.where(_row_iota(x.shape) >= d, pltpu.roll(x, d, 0), 0.0)


def _shift_up(x, d):
    n = x.shape[0]
    if d == 0:
        return x
    if d % SUBLANES == 0:
        return jnp.concatenate([x[d:], jnp.zeros((d,) + x.shape[1:], x.dtype)], axis=0)
    return jnp.where(_row_iota(x.shape) < n - d, pltpu.roll(x, n - d, 0), 0.0)


def _conv3_silu(x_ref, prev_ref, next_ref, w_ref, tap0, bias_row, is_first, is_last):
    x = x_ref[0].astype(F32)
    n = x.shape[0]
    rows = _row_iota(x.shape)
    prev_row = jnp.where(is_first, 0.0, prev_ref[0][SUBLANES - 1:SUBLANES, :].astype(F32))
    next_row = jnp.where(is_last, 0.0, next_ref[0][0:1, :].astype(F32))
    xm = jnp.where(rows == 0, prev_row, pltpu.roll(x, 1, 0))
    xp = jnp.where(rows == n - 1, next_row, pltpu.roll(x, n - 1, 0))
    y = (xm * w_ref[tap0:tap0 + 1, :] + x * w_ref[tap0 + 1:tap0 + 2, :] + xp * w_ref[tap0 + 2:tap0 + 3, :]
         + w_ref[bias_row:bias_row + 1, :])
    return _silu(y)


def _mlstm_body(nc, *refs):
    (qf, kf, vf, qfp, qfn, kfp, kfn, gcf,
     qb, kb, vb, qbp, qbn, kbp, kbn, gcb,
     cw_ref, gbc_ref, tri_ref, c0_ref, n0_ref, m0_ref,
     of_ref, ob_ref, cN_ref, nN_ref, mN_ref, c_sc, n_sc, m_sc) = refs
    i = pl.program_id(1)

    @pl.when(i == 0)
    def _():
        c_sc[...] = c0_ref[0]
        n_sc[...] = n0_ref[0]
        m_sc[...] = m0_ref[0]

    dh = MLSTM_DH
    for d in range(N_DIR):
        q_ref, k_ref, v_ref, qp, qn, kp, kn, gc_ref, o_ref = (
            (qf, kf, vf, qfp, qfn, kfp, kfn, gcf, of_ref) if d == 0 else
            (qb, kb, vb, qbp, qbn, kbp, kbn, gcb, ob_ref))
        chunk = i if d == 0 else nc - 1 - i
        is_first, is_last = chunk == 0, chunk == nc - 1
        q = _conv3_silu(q_ref, qp, qn, cw_ref, 0, 6, is_first, is_last)
        k = _conv3_silu(k_ref, kp, kn, cw_ref, 3, 7, is_first, is_last) * (dh ** -0.5)
        v = v_ref[0]
        tri = tri_ref[d]
        tri_t = tri_ref[1 - d]
        gcol = gc_ref[0] + gbc_ref[...]
        g_hi, g_mid, g_lo = _split3(gcol)
        eye = tri_ref[2]
        grow = (_dot_tn(g_hi, eye) + _dot_tn(g_mid, eye) + _dot_tn(g_lo, eye))[:N_IF, :]
        bcum_c = _dot_exact_lhs(tri, _log_sigmoid(gcol))
        bcum_r = _dot_exact_rhs(_log_sigmoid(grow), tri_t)
        mask = tri.astype(F32) > 0.5
        last = CHUNK - 1 if d == 0 else 0
        q16, k16 = q.astype(BF16), k.astype(BF16)
        outs = []
        for h in range(MLSTM_HEADS):
            ci, cf = d * 8 + h, d * 8 + MLSTM_HEADS + h
            bc = bcum_c[:, cf:cf + 1]
            li_c = gcol[:, ci:ci + 1]
            br = bcum_r[cf:cf + 1, :]
            li_r = grow[ci:ci + 1, :]
            m_st = m_sc[d, h][0:1, 0:1]
            logw = jnp.where(mask, bc - (br - li_r), -jnp.inf)
            m_pos = jnp.maximum(bc + m_st, jnp.max(logw, axis=1, keepdims=True))
            w_intra = jnp.exp(logw - m_pos)
            w_prev = jnp.exp(bc + m_st - m_pos)
            sl = slice(h * dh, (h + 1) * dh)
            qh, kh, vh = q16[:, sl], k16[:, sl], v[:, sl]
            c_st = c_sc[d, h]
            n_st = n_sc[d, h][0:1, :]
            s = _dot_nt(qh, kh) * w_intra
            num = _dot(s.astype(BF16), vh) + w_prev * _dot(qh, c_st.astype(BF16))
            den = jnp.sum(s, axis=1, keepdims=True) + w_prev * jnp.sum(q[:, sl] * n_st, axis=1, keepdims=True)
            outs.append(num / jnp.maximum(jnp.abs(den), jnp.exp(-m_pos)))
            m_next = m_pos[last:last + 1, :]
            bc_last = bc[last:last + 1, :]
            w_state = jnp.exp(bc_last - bc + li_c - m_next)
            decay = jnp.exp(bc_last + m_st - m_next)
            kw = k[:, sl] * w_state
            c_sc[d, h] = decay * c_st + _dot_tn(kw.astype(BF16), vh)
            n_sc[d, h] = jnp.broadcast_to(decay * n_st + jnp.sum(kw, axis=0, keepdims=True), (SUBLANES, dh))
            m_sc[d, h] = jnp.broadcast_to(m_next, (SUBLANES, 128))
        o_ref[0] = jnp.concatenate(outs, axis=1).astype(o_ref.dtype)

    @pl.when(i == nc - 1)
    def _():
        cN_ref[0] = c_sc[...]
        nN_ref[0] = n_sc[...]
        mN_ref[0] = m_sc[...]


def _tri_consts():
    t = np.tril(np.ones((CHUNK, CHUNK), np.float32))
    return jnp.asarray(np.stack([t, t.T, np.eye(CHUNK, dtype=np.float32)]), BF16)


def mlstm_zero_state(b):
    return (jnp.zeros((b, N_DIR, MLSTM_HEADS, MLSTM_DH, MLSTM_DH), F32),
            jnp.zeros((b, N_DIR, MLSTM_HEADS, SUBLANES, MLSTM_DH), F32),
            jnp.zeros((b, N_DIR, MLSTM_HEADS, SUBLANES, 128), F32))


def mlstm_scan(zb, zs, conv_w, conv_b, gate_b, state, col_major):
    b, l, _ = zb.shape
    nc = l // CHUNK
    nblk = ZB_W // W
    oq, ok, ov = OFF["c_q"] // W, OFF["c_k"] // W, OFF["c_v"] // W
    cw8 = jnp.concatenate([conv_w.reshape(6, W), conv_b.reshape(2, W)], axis=0).astype(F32)
    gb = gate_b.astype(F32).reshape(N_IF)
    gbc = jnp.zeros((1, ZS_W), F32).at[0, :N_IF].set(gb)
    if col_major:
        rows = l // GRID_W
        assert rows == CHUNK and GRID_W == nc
        zb_v = zb.reshape(b, rows, GRID_W * ZB_W)
        zs_v = zs.reshape(b, rows, GRID_W * ZS_W)
        last_rb = rows // SUBLANES - 1

        def main(off, dirn):
            return pl.BlockSpec((1, CHUNK, W), lambda bi, i: (bi, 0, _ch(i, dirn) * nblk + off))

        def prev(off, dirn):
            return pl.BlockSpec((1, SUBLANES, W),
                                lambda bi, i: (bi, last_rb, jnp.maximum(_ch(i, dirn) - 1, 0) * nblk + off))

        def nxt(off, dirn):
            return pl.BlockSpec((1, SUBLANES, W),
                                lambda bi, i: (bi, 0, jnp.minimum(_ch(i, dirn) + 1, nc - 1) * nblk + off))

        def gcol(dirn):
            return pl.BlockSpec((1, CHUNK, ZS_W), lambda bi, i: (bi, 0, _ch(i, dirn)))

        def out_spec(dirn):
            return pl.BlockSpec((1, CHUNK, W), lambda bi, i: (bi, 0, _ch(i, dirn)))
        out_shape_o = jax.ShapeDtypeStruct((b, rows, GRID_W * W), BF16)
    else:
        zb_v, zs_v = zb, zs
        cpb = CHUNK // SUBLANES

        def main(off, dirn):
            return pl.BlockSpec((1, CHUNK, W), lambda bi, i: (bi, _ch(i, dirn), off))

        def prev(off, dirn):
            return pl.BlockSpec((1, SUBLANES, W),
                                lambda bi, i: (bi, jnp.maximum(_ch(i, dirn) * cpb - 1, 0), off))

        def nxt(off, dirn):
            return pl.BlockSpec((1, SUBLANES, W),
                                lambda bi, i: (bi, jnp.minimum((_ch(i, dirn) + 1) * cpb, nc * cpb - 1), off))

        def gcol(dirn):
            return pl.BlockSpec((1, CHUNK, ZS_W), lambda bi, i: (bi, _ch(i, dirn), 0))

        def out_spec(dirn):
            return pl.BlockSpec((1, CHUNK, W), lambda bi, i: (bi, _ch(i, dirn), 0))
        out_shape_o = jax.ShapeDtypeStruct((b, l, W), BF16)

    def _ch(i, dirn):
        return i if dirn == 0 else nc - 1 - i

    def full(shape):
        return pl.BlockSpec(shape, lambda bi, i: (0,) * len(shape))

    def per_b(shape):
        return pl.BlockSpec((1,) + shape, lambda bi, i: (bi,) + (0,) * len(shape))

    in_specs, args = [], []
    for dirn in range(N_DIR):
        in_specs += [main(oq, dirn), main(ok, dirn), main(ov, dirn), prev(oq, dirn), nxt(oq, dirn),
                     prev(ok, dirn), nxt(ok, dirn), gcol(dirn)]
        args += [zb_v, zb_v, zb_v, zb_v, zb_v, zb_v, zb_v, zs_v]
    c_shape = (N_DIR, MLSTM_HEADS, MLSTM_DH, MLSTM_DH)
    n_shape = (N_DIR, MLSTM_HEADS, SUBLANES, MLSTM_DH)
    m_shape = (N_DIR, MLSTM_HEADS, SUBLANES, 128)
    in_specs += [full((8, W)), full((1, ZS_W)), full((3, CHUNK, CHUNK)),
                 per_b(c_shape), per_b(n_shape), per_b(m_shape)]
    args += [cw8, gbc, _tri_consts(), *state]
    out_f, out_b, c_n, n_n, m_n = pl.pallas_call(
        functools.partial(_mlstm_body, nc),
        grid=(b, nc),
        in_specs=in_specs,
        out_specs=[out_spec(0), out_spec(1), per_b(c_shape), per_b(n_shape), per_b(m_shape)],
        out_shape=[out_shape_o, out_shape_o,
                   jax.ShapeDtypeStruct((b,) + c_shape, F32), jax.ShapeDtypeStruct((b,) + n_shape, F32),
                   jax.ShapeDtypeStruct((b,) + m_shape, F32)],
        scratch_shapes=[pltpu.VMEM(c_shape, F32), pltpu.VMEM(n_shape, F32), pltpu.VMEM(m_shape, F32)],
        compiler_params=_cparams(2),
        name="mlstm_scan",
    )(*args)
    return out_f.reshape(b, l, W), out_b.reshape(b, l, W), (c_n, n_n, m_n)


GLA_LEVELS = (1, 2, 4, 8, 16, 32)


def _gla_masks():
    t = np.arange(CHUNK)[:, None]
    s = np.arange(CHUNK)[None, :]
    ms = [(t == s)]
    for m in GLA_LEVELS:
        ms.append((t // (2 * m) == s // (2 * m)) & (t % (2 * m) >= m) & (s % (2 * m) < m))
    return jnp.asarray(np.stack(ms).astype(np.float32))


def _cumsum_rows(x):
    d = 1
    while d < x.shape[0]:
        x = x + _shift_down(x, d)
        d *= 2
    return x


def _prev_block_end(x, m):
    n, w = x.shape
    if m >= SUBLANES:
        ends = x.reshape(n // m, m, w)[:, m - 1:m, :]
        prev = jnp.concatenate([jnp.zeros((1, 1, w), x.dtype), ends[:-1]], axis=0)
        return jnp.broadcast_to(prev, (n // m, m, w)).reshape(n, w)
    r = _row_iota(x.shape) % m
    y = _shift_down(x, 1)
    for j in range(1, m):
        y = jnp.where(r == j, _shift_down(x, j + 1), y)
    return y


def _own_block_end(x, m):
    n, w = x.shape
    if m >= SUBLANES:
        ends = x.reshape(n // m, m, w)[:, m - 1:m, :]
        return jnp.broadcast_to(ends, (n // m, m, w)).reshape(n, w)
    r = _row_iota(x.shape) % m
    y = x
    for j in range(m - 1):
        y = jnp.where(r == j, _shift_up(x, m - 1 - j), y)
    return y


def _gla_body(nc, with_out, *refs):
    (qf, kf, vf, af, qb, kb, vb, ab, w2_ref, b2_ref, j_ref, mask_ref, s0_ref,
     of_ref, ob_ref, sN_ref, s_sc) = refs
    i = pl.program_id(1)

    @pl.when(i == 0)
    def _():
        s_sc[...] = s0_ref[0]

    dk, dv = GLA_DK, GLA_DV
    for d in range(N_DIR):
        q_ref, k_ref, v_ref, a_ref, o_ref = (qf, kf, vf, af, of_ref) if d == 0 else (qb, kb, vb, ab, ob_ref)
        q16, k16, v16, a = q_ref[0], k_ref[0], v_ref[0], a_ref[0]
        if d == 1:
            jm = j_ref[...]
            q16 = _dot(jm, q16).astype(BF16)
            k16 = _dot(jm, k16).astype(BF16)
            v16 = _dot(jm, v16).astype(BF16)
            a = _dot_exact_lhs(jm, a)
        la = _log_sigmoid(_dot_f32x3(a, w2_ref[d]) + b2_ref[d]) * (1.0 / GLA_TAU)
        bc = _cumsum_rows(la)
        q = q16.astype(F32) * (dk ** -0.5)
        k = k16.astype(F32)
        b_last = bc[CHUNK - 1:CHUNK, :]
        q_in = (q * jnp.exp(bc)).astype(BF16)
        k_dec = (k * jnp.exp(b_last - bc)).astype(BF16)
        e_last = jnp.exp(b_last)
        if with_out:
            qs, ks = [q.astype(BF16)], [k16]
            for m in GLA_LEVELS:
                qs.append((q * jnp.exp(bc - _prev_block_end(bc, m))).astype(BF16))
                ks.append((k * jnp.exp(_own_block_end(bc, m) - bc)).astype(BF16))
        outs = []
        for h in range(GLA_HEADS):
            sk = slice(h * dk, (h + 1) * dk)
            sv = slice(h * dv, (h + 1) * dv)
            st = s_sc[d, h]
            if with_out:
                att = jnp.zeros((CHUNK, CHUNK), F32)
                for lvl in range(len(GLA_LEVELS) + 1):
                    att = att + mask_ref[lvl] * _dot_nt(qs[lvl][:, sk], ks[lvl][:, sk])
                outs.append(_dot(att.astype(BF16), v16[:, sv]) + _dot_nt(q_in[:, sk], st.astype(BF16)))
            s_sc[d, h] = st * e_last[:, sk] + _dot_tn(v16[:, sv], k_dec[:, sk])
        if with_out:
            o = jnp.concatenate(outs, axis=1).astype(BF16)
            if d == 1:
                o = _dot(j_ref[...], o).astype(BF16)
            o_ref[0] = o
        else:
            o_ref[0] = jnp.zeros(o_ref.shape[1:], o_ref.dtype)

    @pl.when(i == nc - 1)
    def _():
        sN_ref[0] = s_sc[...]


def gla_zero_state(b):
    return jnp.zeros((b, N_DIR, GLA_HEADS, GLA_DV, GLA_DK), F32)


def gla_scan(zb, zs, w2, b2, state, with_out=True):
    b, l, _ = zb.shape
    nc = l // CHUNK
    hk = GLA_HEADS * GLA_DK
    oq, ok, ov = OFF["d_q"] // hk, OFF["d_k"] // hk, OFF["d_v"] // W
    w2p = jnp.zeros((N_DIR, ZS_W, hk), F32)
    for d in range(N_DIR):
        w2p = w2p.at[d, N_IF + d * GLA_RANK:N_IF + (d + 1) * GLA_RANK, :].set(w2[d].astype(F32))
    b2r = b2.astype(F32).reshape(N_DIR, 1, hk)
    jm = jnp.asarray(np.eye(CHUNK, dtype=np.float32)[::-1].copy(), BF16)

    def _ch(i, dirn):
        return i if dirn == 0 else nc - 1 - i

    def blk(width, off, dirn):
        return pl.BlockSpec((1, CHUNK, width), lambda bi, i: (bi, _ch(i, dirn), off))

    def full(shape):
        return pl.BlockSpec(shape, lambda bi, i: (0,) * len(shape))

    s_shape = (N_DIR, GLA_HEADS, GLA_DV, GLA_DK)
    s_spec = pl.BlockSpec((1,) + s_shape, lambda bi, i: (bi, 0, 0, 0, 0))
    in_specs, args = [], []
    for dirn in range(N_DIR):
        in_specs += [blk(hk, oq, dirn), blk(hk, ok, dirn), blk(W, ov, dirn), blk(ZS_W, 0, dirn)]
        args += [zb, zb, zb, zs]
    in_specs += [full((N_DIR, ZS_W, hk)), full((N_DIR, 1, hk)), full((CHUNK, CHUNK)),
                 full((len(GLA_LEVELS) + 1, CHUNK, CHUNK)), s_spec]
    args += [w2p, b2r, jm, _gla_masks(), state]
    o_shape = jax.ShapeDtypeStruct((b, l, W), BF16)
    out_f, out_b, s_n = pl.pallas_call(
        functools.partial(_gla_body, nc, with_out),
        grid=(b, nc),
        in_specs=in_specs,
        out_specs=[blk(W, 0, 0), blk(W, 0, 1), s_spec],
        out_shape=[o_shape, o_shape, jax.ShapeDtypeStruct((b,) + s_shape, F32)],
        scratch_shapes=[pltpu.VMEM(s_shape, F32)],
        compiler_params=_cparams(2),
        name="gla_scan",
    )(*args)
    return out_f, out_b, s_n


def _matmul_body(a_ref, b_ref, o_ref):
    o_ref[...] = jnp.dot(a_ref[...], b_ref[...], preferred_element_type=F32).astype(o_ref.dtype)


def _pick_tile(n, cands):
    for c in cands:
        if n % c == 0:
            return c
    raise ValueError(f"no tile for {n}")


def pallas_matmul(a, b, out_dtype=F32):
    m, k = a.shape
    k2, n = b.shape
    assert k == k2
    n_pad = -(-n // 128) * 128
    if n_pad != n:
        b = jnp.pad(b, ((0, 0), (0, n_pad - n)))
    m_pad = -(-m // 16) * 16
    if m_pad != m:
        a = jnp.pad(a, ((0, m_pad - m), (0, 0)))
    tm_cands = (1024, 512, 256, 128, 64, 32, 16) if k <= 4096 else (512, 256, 128, 64, 32, 16)
    tm = _pick_tile(m_pad, tm_cands)
    tn = _pick_tile(n_pad, (512, 256, 128))
    out = pl.pallas_call(
        _matmul_body,
        grid=(m_pad // tm, n_pad // tn),
        in_specs=[pl.BlockSpec((tm, k), lambda i, j: (i, 0)),
                  pl.BlockSpec((k, tn), lambda i, j: (0, j))],
        out_specs=pl.BlockSpec((tm, tn), lambda i, j: (i, j)),
        out_shape=jax.ShapeDtypeStruct((m_pad, n_pad), out_dtype),
        compiler_params=_cparams(2),
        name="proj_matmul",
    )(a.astype(BF16), b.astype(BF16))
    if m_pad != m or n_pad != n:
        out = out[:m, :n]
    return out


def _bmm_body(a_ref, b_ref, o_ref):
    o_ref[0] = jnp.dot(a_ref[...], b_ref[0], preferred_element_type=F32).astype(o_ref.dtype)


def pallas_bmm(a, x, out_dtype=F32):
    m, k = a.shape
    bsz, k2, n = x.shape
    assert k == k2
    tm = _pick_tile(m, (512, 256, 128))
    tn = _pick_tile(n, (512, 256, 128))
    return pl.pallas_call(
        _bmm_body,
        grid=(bsz, m // tm, n // tn),
        in_specs=[pl.BlockSpec((tm, k), lambda bi, i, j: (i, 0)),
                  pl.BlockSpec((1, k, tn), lambda bi, i, j: (bi, 0, j))],
        out_specs=pl.BlockSpec((1, tm, tn), lambda bi, i, j: (bi, i, j)),
        out_shape=jax.ShapeDtypeStruct((bsz, m, n), out_dtype),
        compiler_params=_cparams(3),
        name="dft_matmul",
    )(a.astype(BF16), x.astype(BF16))


def mm(a, b, out_dtype=F32):
    lead = a.shape[:-1]
    out = pallas_matmul(a.reshape(-1, a.shape[-1]), b, out_dtype)
    return out.reshape(*lead, b.shape[-1])


def rmsnorm(x, g):
    xf = x.astype(F32)
    y = xf * lax.rsqrt(jnp.mean(xf * xf, axis=-1, keepdims=True) + EPS)
    return (y * g.astype(F32)).astype(x.dtype)


def repack_w_in(w_in):
    wbig = jnp.concatenate([w_in[:, :ORIG_IF], w_in[:, ORIG_IF + N_IF:ORIG_DA]], axis=1)
    wsmall = jnp.concatenate([w_in[:, ORIG_IF:ORIG_IF + N_IF], w_in[:, ORIG_DA:ORIG_DA + N_DA]], axis=1)
    wsmall = jnp.pad(wsmall, ((0, 0), (0, ZS_W - N_IF - N_DA)))
    return wbig.astype(BF16), wsmall.astype(BF16)


def short_conv(x, w, b):
    xp = jnp.pad(x, ((0, 0), (1, 1), (0, 0)))
    return xp[:, :-2] * w[0] + xp[:, 1:-1] * w[1] + xp[:, 2:] * w[2] + b


DFT_BLK = 64


def dft_cos_sin(n_rows, n_cols, modulus):
    s = jnp.arange(n_cols, dtype=jnp.int32)[None, :]
    f0 = jnp.arange(DFT_BLK, dtype=jnp.int32)[:, None]
    f1 = jnp.arange(n_rows // DFT_BLK, dtype=jnp.int32)[:, None] * DFT_BLK

    def tab(f):
        ang = ((f * s) % modulus).astype(F32) * (2.0 * math.pi / modulus)
        return jnp.cos(ang), jnp.sin(ang)
    ca, sa = tab(f1)
    cb, sb = tab(f0)
    c = ca[:, None, :] * cb[None] - sa[:, None, :] * sb[None]
    sn = sa[:, None, :] * cb[None] + ca[:, None, :] * sb[None]
    return c.reshape(n_rows, n_cols), sn.reshape(n_rows, n_cols)


def fourier_mats(l):
    c, sn = dft_cos_sin(l, l, l)
    gf = jnp.concatenate([c, -sn], axis=1) * ((l * FNET_GW) ** -0.5)
    cg, sg = dft_cos_sin(FNET_GW, FNET_GW, FNET_GW)
    eye = jnp.eye(FNET_GROUPS, dtype=F32)
    bd = jnp.concatenate([jnp.kron(eye, cg), jnp.kron(eye, sg)], axis=1)
    return gf.astype(BF16), bd.astype(BF16)


def fourier_mix(u, w, mats):
    gf, bd = mats
    b, l, _ = u.shape
    p = mm(u, bd, BF16)
    pq = jnp.concatenate([p[..., :W], p[..., W:]], axis=1)
    return mm(pallas_bmm(gf, pq, BF16), w)


def hyena_mats(l):
    c, sn = dft_cos_sin(l, l, 2 * l)
    alt = jnp.where(jnp.arange(l) % 2 == 0, 1.0, -1.0).astype(F32)
    g = jnp.concatenate([c, sn.at[0].set(alt)], axis=0)
    return g.astype(BF16), g.T.astype(BF16), alt


def hyena_filter_coeffs(l, p, mats):
    g, _, alt = mats
    t = jnp.linspace(0.0, 1.0, l, dtype=F32)[:, None]
    bands = jnp.linspace(1e-4, HYENA_BANDS - 1, HYENA_BANDS, dtype=F32)[None, :]
    ang = (2.0 * math.pi / l) * jnp.arange(l, dtype=F32)[:, None] * bands
    feats = jnp.concatenate([t, jnp.cos(ang), -jnp.sin(ang)], axis=-1)
    fr = p["hy_f_freq"].astype(F32)
    hid = jnp.sin(fr[0] * (feats @ p["hy_f_w1"].astype(F32) + p["hy_f_b1"].astype(F32)))
    hid = jnp.sin(fr[1] * (hid @ p["hy_f_w2"].astype(F32) + p["hy_f_b2"].astype(F32)))
    filt = (hid @ p["hy_f_w3"].astype(F32)).reshape(l, HYENA_ORDER, N_DIR, GROUP_W)
    deltas = jnp.abs(jnp.linspace(HYENA_SLOW_DECAY, HYENA_FAST_DECAY, GROUP_W, dtype=F32))
    filt = filt * jnp.exp(-t * deltas)[:, None, None, :]
    fwd, bwd = filt[:, :, 0], filt[:, :, 1]
    bwd = bwd.at[0].set(0.0)
    norm = jnp.sum(jnp.abs(fwd), axis=0, keepdims=True) + jnp.sum(jnp.abs(bwd), axis=0, keepdims=True) + EPS
    n_col = HYENA_ORDER * GROUP_W
    fp = ((fwd + bwd) / norm).reshape(l, n_col)
    fm = ((fwd - bwd) / norm).reshape(l, n_col)
    kre = pallas_matmul(g[:l], fp)
    kim = -pallas_matmul(g[l:], fm)
    k_nyq = jnp.sum(alt[:, None] * fp, axis=0)
    row0 = (jnp.arange(l) == 0)[:, None]
    kim = jnp.where(row0, 0.0, kim)
    a4 = jnp.where(row0, k_nyq[None, :], kre)
    coef = jnp.where(row0, 0.5 / l, 1.0 / l)
    return tuple((coef * a).reshape(l, HYENA_ORDER, GROUP_W) for a in (kre, kim, a4))


def long_conv(z, mats, ck):
    g, gt, _ = mats
    l = z.shape[1]
    zhat = pallas_bmm(g, z)
    zc, zs = zhat[:, :l], zhat[:, l:]
    kre, kim, a4 = ck
    y = jnp.concatenate([zc * kre + zs * kim, zs * a4 - zc * kim], axis=1)
    return pallas_bmm(gt, y)


def hyena_mix(u, conv_w, conv_b, coeffs, bias, mats):
    u = short_conv(u, conv_w, conv_b).astype(F32)
    v, x1, x2 = jnp.split(u, 3, axis=-1)
    bias = bias.astype(F32)
    z = x1 * (long_conv(v, mats, tuple(a[:, 0] for a in coeffs)) + bias[0] * v)
    return x2 * (long_conv(z, mats, tuple(a[:, 1] for a in coeffs)) + bias[1] * z)


def gla_out(o, g):
    b, l, _ = o.shape
    oh = o.reshape(b, l, GLA_HEADS, GLA_DV)
    oh = oh * lax.rsqrt(jnp.mean(oh * oh, axis=-1, keepdims=True) + EPS)
    return oh.reshape(b, l, GROUP_W) * g.astype(F32)


def zcol(zb, name, width):
    return zb[..., OFF[name]:OFF[name] + width].astype(F32)


def combine(zb, o_m, o_g, p, mats):
    l = zb.shape[1]
    f_mats, h_mats = mats

    def silu(a):
        return jax.nn.silu(a)
    y_a = fourier_mix(zcol(zb, "a_u", W), p["fnet_w"], f_mats) * silu(zcol(zb, "a_g", W))
    y_b = hyena_mix(zcol(zb, "b_u", 3 * W), p["hy_conv_w"], p["hy_conv_b"], hyena_filter_coeffs(l, p, h_mats),
                    p["hy_bias"], h_mats) * silu(zcol(zb, "b_g", W))
    y_c = o_m * jax.nn.sigmoid(zcol(zb, "c_o", W)) * silu(zcol(zb, "c_g", W))
    y_d = gla_out(o_g, p["gla_norm_g"]) * silu(zcol(zb, "d_g", W))
    y = jnp.concatenate([y_a, y_b, y_c, y_d], axis=-1)
    return mm(y, p["w_out"])


def mixer_layer(x, xc, c, c_ctx, p, last, mats, mats_c):
    b, l, d = x.shape
    mod3 = mm(jnp.concatenate([jax.nn.silu(c), jax.nn.silu(c_ctx)[None]], axis=0), p["ada_w"]) + p["ada_b"]
    shift, scale, gate = jnp.split(mod3[:b], 3, axis=-1)
    mod_c = mod3[b]
    h = rmsnorm(x, p["norm_g"]) * (1 + scale[:, None]) + shift[:, None]
    hc = rmsnorm(xc, p["norm_g"]) * (1 + mod_c[d:2 * d]) + mod_c[:d]
    wbig, wsmall = repack_w_in(p["w_in"])
    zb = mm(h, wbig, BF16)
    zs = mm(h, wsmall)
    zcb = mm(hc, wbig, BF16)
    zcs = mm(hc, wsmall)

    cmf, cmb, m_states = mlstm_scan(zcb, zcs, p["ml_conv_w"], p["ml_conv_b"], p["ml_gate_b"],
                                    mlstm_zero_state(b), col_major=False)
    lmf, lmb, _ = mlstm_scan(zb, zs, p["ml_conv_w"], p["ml_conv_b"], p["ml_gate_b"], m_states, col_major=True)
    lat_m = lmf.astype(F32) + lmb.astype(F32)

    cgf, cgb, g_states = gla_scan(zcb, zcs, p["gla_w2"], p["gla_b2"], gla_zero_state(b))
    lgf, lgb, _ = gla_scan(zb, zs, p["gla_w2"], p["gla_b2"], g_states)
    lat_g = lgf.astype(F32) + lgb.astype(F32)

    x = x + gate[:, None] * combine(zb, lat_m, lat_g, p, mats)
    if not last:
        ctx_m = cmf.astype(F32) + cmb.astype(F32)
        ctx_g = cgf.astype(F32) + cgb.astype(F32)
        xc = xc + mod_c[2 * d:] * combine(zcb, ctx_m, ctx_g, p, mats_c)
    return x, xc


def kernel(x, c, ctx, c_ctx, ada_w, ada_b, norm_g, w_in, fnet_w, hy_conv_w, hy_conv_b, hy_f_w1, hy_f_b1,
           hy_f_w2, hy_f_b2, hy_f_w3, hy_f_freq, hy_bias, ml_conv_w, ml_conv_b, ml_gate_b, gla_w2, gla_b2,
           gla_norm_g, w_out, final_g):
    xc = ctx
    depth = ada_w.shape[0]
    mats = (fourier_mats(x.shape[1]), hyena_mats(x.shape[1]))
    mats_c = (fourier_mats(ctx.shape[1]), hyena_mats(ctx.shape[1]))
    for i in range(depth):
        p = {
            "ada_w": ada_w[i], "ada_b": ada_b[i], "norm_g": norm_g[i], "w_in": w_in[i], "fnet_w": fnet_w[i],
            "hy_conv_w": hy_conv_w[i], "hy_conv_b": hy_conv_b[i], "hy_f_w1": hy_f_w1[i], "hy_f_b1": hy_f_b1[i],
            "hy_f_w2": hy_f_w2[i], "hy_f_b2": hy_f_b2[i], "hy_f_w3": hy_f_w3[i], "hy_f_freq": hy_f_freq[i],
            "hy_bias": hy_bias[i], "ml_conv_w": ml_conv_w[i], "ml_conv_b": ml_conv_b[i], "ml_gate_b": ml_gate_b[i],
            "gla_w2": gla_w2[i], "gla_b2": gla_b2[i], "gla_norm_g": gla_norm_g[i], "w_out": w_out[i],
        }
        x, xc = mixer_layer(x, xc, c, c_ctx, p, i == depth - 1, mats, mats_c)
    return rmsnorm(x, final_g)
```

```python
import functools
import math

import numpy as np
import jax
import jax.numpy as jnp
from jax import lax
from jax.experimental import pallas as pl
from jax.experimental.pallas import tpu as pltpu

D_MODEL = 4096
GRID_W = 64
N_GROUPS = 4
GROUP_W = D_MODEL // N_GROUPS
FNET_GROUPS = 4
FNET_GW = GROUP_W // FNET_GROUPS
HYENA_ORDER = 2
HYENA_BANDS = 16
HYENA_FAST_DECAY = math.log(1e-2) / 0.3
HYENA_SLOW_DECAY = math.log(1e-2) / 1.5
MLSTM_HEADS = 4
MLSTM_DH = GROUP_W // MLSTM_HEADS
GLA_HEADS = 4
GLA_DV = GROUP_W // GLA_HEADS
GLA_DK = GLA_DV // 2
GLA_RANK = 16
GLA_TAU = 16.0
N_DIR = 2
CHUNK = 64
EPS = 1e-6
F32 = jnp.float32
BF16 = jnp.bfloat16

W = GROUP_W
OFF = {"a_u": 0, "a_g": W, "b_u": 2 * W, "b_g": 5 * W, "c_q": 6 * W, "c_k": 7 * W, "c_v": 8 * W,
       "c_o": 9 * W, "c_g": 10 * W, "d_q": 11 * W, "d_k": 11 * W + 512, "d_v": 12 * W, "d_g": 13 * W}
ZB_W = 14 * W
ZS_W = 128
N_IF = N_DIR * 2 * MLSTM_HEADS
N_DA = N_DIR * GLA_RANK
ORIG_IF = 11 * W
ORIG_DA = ORIG_IF + N_IF + 2 * 512 + 2 * W

V7X_VMEM_LIMIT_BYTES = 56 * 1024 * 1024
SUBLANES = 8


def _cparams(n_axes):
    return pltpu.CompilerParams(dimension_semantics=("arbitrary",) * n_axes,
                                vmem_limit_bytes=V7X_VMEM_LIMIT_BYTES)


def _split3(a):
    hi = a.astype(BF16)
    r1 = a - hi.astype(F32)
    mid = r1.astype(BF16)
    lo = (r1 - mid.astype(F32)).astype(BF16)
    return hi, mid, lo


def _dot(a, b):
    return jnp.dot(a, b, preferred_element_type=F32)


def _dot_nt(a, b):
    return lax.dot_general(a, b, (((1,), (1,)), ((), ())), preferred_element_type=F32)


def _dot_tn(a, b):
    return lax.dot_general(a, b, (((0,), (0,)), ((), ())), preferred_element_type=F32)


def _dot_exact_lhs(m_bf16, x):
    hi, mid, lo = _split3(x)
    return _dot(m_bf16, hi) + _dot(m_bf16, mid) + _dot(m_bf16, lo)


def _dot_exact_rhs(x, m_bf16):
    hi, mid, lo = _split3(x)
    return _dot(hi, m_bf16) + _dot(mid, m_bf16) + _dot(lo, m_bf16)


def _dot_f32x3(a, b):
    ah = a.astype(BF16)
    al = (a - ah.astype(F32)).astype(BF16)
    bh = b.astype(BF16)
    bl = (b - bh.astype(F32)).astype(BF16)
    return _dot(ah, bh) + _dot(ah, bl) + _dot(al, bh)


def _log_sigmoid(x):
    return jnp.minimum(x, 0.0) - jnp.log1p(jnp.exp(-jnp.abs(x)))


def _silu(x):
    return x * jax.nn.sigmoid(x)


def _row_iota(shape):
    return lax.broadcasted_iota(jnp.int32, shape, 0)


def _shift_down(x, d):
    n = x.shape[0]
    if d % SUBLANES == 0:
        return jnp.concatenate([jnp.zeros((d,) + x.shape[1:], x.dtype), x[: n - d]], axis=0)
    return jnp.where(_row_iota(x.shape) >= d, pltpu.roll(x, d, 0), 0.0)


def _shift_up(x, d):
    n = x.shape[0]
    if d == 0:
        return x
    if d % SUBLANES == 0:
        return jnp.concatenate([x[d:], jnp.zeros((d,) + x.shape[1:], x.dtype)], axis=0)
    return jnp.where(_row_iota(x.shape) < n - d, pltpu.roll(x, n - d, 0), 0.0)


def _conv3_silu(x_ref, prev_ref, next_ref, w_ref, tap0, bias_row, is_first, is_last):
    x = x_ref[0].astype(F32)
    n = x.shape[0]
    rows = _row_iota(x.shape)
    prev_row = jnp.where(is_first, 0.0, prev_ref[0][SUBLANES - 1:SUBLANES, :].astype(F32))
    next_row = jnp.where(is_last, 0.0, next_ref[0][0:1, :].astype(F32))
    xm = jnp.where(rows == 0, prev_row, pltpu.roll(x, 1, 0))
    xp = jnp.where(rows == n - 1, next_row, pltpu.roll(x, n - 1, 0))
    y = (xm * w_ref[tap0:tap0 + 1, :] + x * w_ref[tap0 + 1:tap0 + 2, :] + xp * w_ref[tap0 + 2:tap0 + 3, :]
         + w_ref[bias_row:bias_row + 1, :])
    return _silu(y)


def _mlstm_body(nc, *refs):
    (qf, kf, vf, qfp, qfn, kfp, kfn, gcf,
     qb, kb, vb, qbp, qbn, kbp, kbn, gcb,
     cw_ref, gbc_ref, tri_ref, c0_ref, n0_ref, m0_ref,
     of_ref, ob_ref, cN_ref, nN_ref, mN_ref, c_sc, n_sc, m_sc) = refs
    i = pl.program_id(1)

    @pl.when(i == 0)
    def _():
        c_sc[...] = c0_ref[0]
        n_sc[...] = n0_ref[0]
        m_sc[...] = m0_ref[0]

    dh = MLSTM_DH
    for d in range(N_DIR):
        q_ref, k_ref, v_ref, qp, qn, kp, kn, gc_ref, o_ref = (
            (qf, kf, vf, qfp, qfn, kfp, kfn, gcf, of_ref) if d == 0 else
            (qb, kb, vb, qbp, qbn, kbp, kbn, gcb, ob_ref))
        chunk = i if d == 0 else nc - 1 - i
        is_first, is_last = chunk == 0, chunk == nc - 1
        q = _conv3_silu(q_ref, qp, qn, cw_ref, 0, 6, is_first, is_last)
        k = _conv3_silu(k_ref, kp, kn, cw_ref, 3, 7, is_first, is_last) * (dh ** -0.5)
        v = v_ref[0]
        tri = tri_ref[d]
        tri_t = tri_ref[1 - d]
        gcol = gc_ref[0] + gbc_ref[...]
        g_hi, g_mid, g_lo = _split3(gcol)
        eye = tri_ref[2]
        grow = (_dot_tn(g_hi, eye) + _dot_tn(g_mid, eye) + _dot_tn(g_lo, eye))[:N_IF, :]
        bcum_c = _dot_exact_lhs(tri, _log_sigmoid(gcol))
        bcum_r = _dot_exact_rhs(_log_sigmoid(grow), tri_t)
        mask = tri.astype(F32) > 0.5
        last = CHUNK - 1 if d == 0 else 0
        q16, k16 = q.astype(BF16), k.astype(BF16)
        outs = []
        for h in range(MLSTM_HEADS):
            ci, cf = d * 8 + h, d * 8 + MLSTM_HEADS + h
            bc = bcum_c[:, cf:cf + 1]
            li_c = gcol[:, ci:ci + 1]
            br = bcum_r[cf:cf + 1, :]
            li_r = grow[ci:ci + 1, :]
            m_st = m_sc[d, h][0:1, 0:1]
            logw = jnp.where(mask, bc - (br - li_r), -jnp.inf)
            m_pos = jnp.maximum(bc + m_st, jnp.max(logw, axis=1, keepdims=True))
            w_intra = jnp.exp(logw - m_pos)
            w_prev = jnp.exp(bc + m_st - m_pos)
            sl = slice(h * dh, (h + 1) * dh)
            qh, kh, vh = q16[:, sl], k16[:, sl], v[:, sl]
            c_st = c_sc[d, h]
            n_st = n_sc[d, h][0:1, :]
            s = _dot_nt(qh, kh) * w_intra
            num = _dot(s.astype(BF16), vh) + w_prev * _dot(qh, c_st.astype(BF16))
            den = jnp.sum(s, axis=1, keepdims=True) + w_prev * jnp.sum(q[:, sl] * n_st, axis=1, keepdims=True)
            outs.append(num / jnp.maximum(jnp.abs(den), jnp.exp(-m_pos)))
            m_next = m_pos[last:last + 1, :]
            bc_last = bc[last:last + 1, :]
            w_state = jnp.exp(bc_last - bc + li_c - m_next)
            decay = jnp.exp(bc_last + m_st - m_next)
            kw = k[:, sl] * w_state
            c_sc[d, h] = decay * c_st + _dot_tn(kw.astype(BF16), vh)
            n_sc[d, h] = jnp.broadcast_to(decay * n_st + jnp.sum(kw, axis=0, keepdims=True), (SUBLANES, dh))
            m_sc[d, h] = jnp.broadcast_to(m_next, (SUBLANES, 128))
        o_ref[0] = jnp.concatenate(outs, axis=1).astype(o_ref.dtype)

    @pl.when(i == nc - 1)
    def _():
        cN_ref[0] = c_sc[...]
        nN_ref[0] = n_sc[...]
        mN_ref[0] = m_sc[...]


def _tri_consts():
    t = np.tril(np.ones((CHUNK, CHUNK), np.float32))
    return jnp.asarray(np.stack([t, t.T, np.eye(CHUNK, dtype=np.float32)]), BF16)


def mlstm_zero_state(b):
    return (jnp.zeros((b, N_DIR, MLSTM_HEADS, MLSTM_DH, MLSTM_DH), F32),
            jnp.zeros((b, N_DIR, MLSTM_HEADS, SUBLANES, MLSTM_DH), F32),
            jnp.zeros((b, N_DIR, MLSTM_HEADS, SUBLANES, 128), F32))


def mlstm_scan(zb, zs, conv_w, conv_b, gate_b, state, col_major):
    b, l, _ = zb.shape
    nc = l // CHUNK
    nblk = ZB_W // W
    oq, ok, ov = OFF["c_q"] // W, OFF["c_k"] // W, OFF["c_v"] // W
    cw8 = jnp.concatenate([conv_w.reshape(6, W), conv_b.reshape(2, W)], axis=0).astype(F32)
    gb = gate_b.astype(F32).reshape(N_IF)
    gbc = jnp.zeros((1, ZS_W), F32).at[0, :N_IF].set(gb)
    if col_major:
        rows = l // GRID_W
        assert rows == CHUNK and GRID_W == nc
        assert ok == oq + 1 and ov == oq + 2
        nblk = 3
        zb_v = zb[..., oq * W:(oq + nblk) * W].reshape(b, rows, GRID_W * nblk * W)
        oq, ok, ov = 0, 1, 2
        zs_v = zs.reshape(b, rows, GRID_W * ZS_W)
        last_rb = rows // SUBLANES - 1

        def main(off, dirn):
            return pl.BlockSpec((1, CHUNK, W), lambda bi, i: (bi, 0, _ch(i, dirn) * nblk + off))

        def prev(off, dirn):
            return pl.BlockSpec((1, SUBLANES, W),
                                lambda bi, i: (bi, last_rb, jnp.maximum(_ch(i, dirn) - 1, 0) * nblk + off))

        def nxt(off, dirn):
            return pl.BlockSpec((1, SUBLANES, W),
                                lambda bi, i: (bi, 0, jnp.minimum(_ch(i, dirn) + 1, nc - 1) * nblk + off))

        def gcol(dirn):
            return pl.BlockSpec((1, CHUNK, ZS_W), lambda bi, i: (bi, 0, _ch(i, dirn)))

        def out_spec(dirn):
            return pl.BlockSpec((1, CHUNK, W), lambda bi, i: (bi, 0, _ch(i, dirn)))
        out_shape_o = jax.ShapeDtypeStruct((b, rows, GRID_W * W), BF16)
    else:
        zb_v, zs_v = zb, zs
        cpb = CHUNK // SUBLANES

        def main(off, dirn):
            return pl.BlockSpec((1, CHUNK, W), lambda bi, i: (bi, _ch(i, dirn), off))

        def prev(off, dirn):
            return pl.BlockSpec((1, SUBLANES, W),
                                lambda bi, i: (bi, jnp.maximum(_ch(i, dirn) * cpb - 1, 0), off))

        def nxt(off, dirn):
            return pl.BlockSpec((1, SUBLANES, W),
                                lambda bi, i: (bi, jnp.minimum((_ch(i, dirn) + 1) * cpb, nc * cpb - 1), off))

        def gcol(dirn):
            return pl.BlockSpec((1, CHUNK, ZS_W), lambda bi, i: (bi, _ch(i, dirn), 0))

        def out_spec(dirn):
            return pl.BlockSpec((1, CHUNK, W), lambda bi, i: (bi, _ch(i, dirn), 0))
        out_shape_o = jax.ShapeDtypeStruct((b, l, W), BF16)

    def _ch(i, dirn):
        return i if dirn == 0 else nc - 1 - i

    def full(shape):
        return pl.BlockSpec(shape, lambda bi, i: (0,) * len(shape))

    def per_b(shape):
        return pl.BlockSpec((1,) + shape, lambda bi, i: (bi,) + (0,) * len(shape))

    in_specs, args = [], []
    for dirn in range(N_DIR):
        in_specs += [main(oq, dirn), main(ok, dirn), main(ov, dirn), prev(oq, dirn), nxt(oq, dirn),
                     prev(ok, dirn), nxt(ok, dirn), gcol(dirn)]
        args += [zb_v, zb_v, zb_v, zb_v, zb_v, zb_v, zb_v, zs_v]
    c_shape = (N_DIR, MLSTM_HEADS, MLSTM_DH, MLSTM_DH)
    n_shape = (N_DIR, MLSTM_HEADS, SUBLANES, MLSTM_DH)
    m_shape = (N_DIR, MLSTM_HEADS, SUBLANES, 128)
    in_specs += [full((8, W)), full((1, ZS_W)), full((3, CHUNK, CHUNK)),
                 per_b(c_shape), per_b(n_shape), per_b(m_shape)]
    args += [cw8, gbc, _tri_consts(), *state]
    out_f, out_b, c_n, n_n, m_n = pl.pallas_call(
        functools.partial(_mlstm_body, nc),
        grid=(b, nc),
        in_specs=in_specs,
        out_specs=[out_spec(0), out_spec(1), per_b(c_shape), per_b(n_shape), per_b(m_shape)],
        out_shape=[out_shape_o, out_shape_o,
                   jax.ShapeDtypeStruct((b,) + c_shape, F32), jax.ShapeDtypeStruct((b,) + n_shape, F32),
                   jax.ShapeDtypeStruct((b,) + m_shape, F32)],
        scratch_shapes=[pltpu.VMEM(c_shape, F32), pltpu.VMEM(n_shape, F32), pltpu.VMEM(m_shape, F32)],
        compiler_params=_cparams(2),
        name="mlstm_scan",
    )(*args)
    return out_f.reshape(b, l, W), out_b.reshape(b, l, W), (c_n, n_n, m_n)


GLA_LEVELS = (1, 2, 4, 8, 16, 32)


def _gla_masks():
    t = np.arange(CHUNK)[:, None]
    s = np.arange(CHUNK)[None, :]
    ms = [(t == s)]
    for m in GLA_LEVELS:
        ms.append((t // (2 * m) == s // (2 * m)) & (t % (2 * m) >= m) & (s % (2 * m) < m))
    return jnp.asarray(np.stack(ms).astype(np.float32))


def _cumsum_rows(x):
    d = 1
    while d < x.shape[0]:
        x = x + _shift_down(x, d)
        d *= 2
    return x


def _prev_block_end(x, m):
    n, w = x.shape
    if m >= SUBLANES:
        ends = x.reshape(n // m, m, w)[:, m - 1:m, :]
        prev = jnp.concatenate([jnp.zeros((1, 1, w), x.dtype), ends[:-1]], axis=0)
        return jnp.broadcast_to(prev, (n // m, m, w)).reshape(n, w)
    r = _row_iota(x.shape) % m
    y = _shift_down(x, 1)
    for j in range(1, m):
        y = jnp.where(r == j, _shift_down(x, j + 1), y)
    return y


def _own_block_end(x, m):
    n, w = x.shape
    if m >= SUBLANES:
        ends = x.reshape(n // m, m, w)[:, m - 1:m, :]
        return jnp.broadcast_to(ends, (n // m, m, w)).reshape(n, w)
    r = _row_iota(x.shape) % m
    y = x
    for j in range(m - 1):
        y = jnp.where(r == j, _shift_up(x, m - 1 - j), y)
    return y


def _gla_body(nc, with_out, *refs):
    (qf, kf, vf, af, qb, kb, vb, ab, w2_ref, b2_ref, j_ref, mask_ref, s0_ref,
     of_ref, ob_ref, sN_ref, s_sc) = refs
    i = pl.program_id(1)

    @pl.when(i == 0)
    def _():
        s_sc[...] = s0_ref[0]

    dk, dv = GLA_DK, GLA_DV
    for d in range(N_DIR):
        q_ref, k_ref, v_ref, a_ref, o_ref = (qf, kf, vf, af, of_ref) if d == 0 else (qb, kb, vb, ab, ob_ref)
        q16, k16, v16, a = q_ref[0], k_ref[0], v_ref[0], a_ref[0]
        if d == 1:
            jm = j_ref[...]
            q16 = _dot(jm, q16).astype(BF16)
            k16 = _dot(jm, k16).astype(BF16)
            v16 = _dot(jm, v16).astype(BF16)
            a = _dot_exact_lhs(jm, a)
        la = _log_sigmoid(_dot_f32x3(a, w2_ref[d]) + b2_ref[d]) * (1.0 / GLA_TAU)
        bc = _cumsum_rows(la)
        q = q16.astype(F32) * (dk ** -0.5)
        k = k16.astype(F32)
        b_last = bc[CHUNK - 1:CHUNK, :]
        q_in = (q * jnp.exp(bc)).astype(BF16)
        k_dec = (k * jnp.exp(b_last - bc)).astype(BF16)
        e_last = jnp.exp(b_last)
        if with_out:
            qs, ks = [q.astype(BF16)], [k16]
            for m in GLA_LEVELS:
                qs.append((q * jnp.exp(bc - _prev_block_end(bc, m))).astype(BF16))
                ks.append((k * jnp.exp(_own_block_end(bc, m) - bc)).astype(BF16))
        outs = []
        for h in range(GLA_HEADS):
            sk = slice(h * dk, (h + 1) * dk)
            sv = slice(h * dv, (h + 1) * dv)
            st = s_sc[d, h]
            if with_out:
                att = jnp.zeros((CHUNK, CHUNK), F32)
                for lvl in range(len(GLA_LEVELS) + 1):
                    att = att + mask_ref[lvl] * _dot_nt(qs[lvl][:, sk], ks[lvl][:, sk])
                outs.append(_dot(att.astype(BF16), v16[:, sv]) + _dot_nt(q_in[:, sk], st.astype(BF16)))
            s_sc[d, h] = st * e_last[:, sk] + _dot_tn(v16[:, sv], k_dec[:, sk])
        if with_out:
            o = jnp.concatenate(outs, axis=1).astype(BF16)
            if d == 1:
                o = _dot(j_ref[...], o).astype(BF16)
            o_ref[0] = o
        else:
            o_ref[0] = jnp.zeros(o_ref.shape[1:], o_ref.dtype)

    @pl.when(i == nc - 1)
    def _():
        sN_ref[0] = s_sc[...]


def gla_zero_state(b):
    return jnp.zeros((b, N_DIR, GLA_HEADS, GLA_DV, GLA_DK), F32)


def gla_scan(zb, zs, w2, b2, state, with_out=True):
    b, l, _ = zb.shape
    nc = l // CHUNK
    hk = GLA_HEADS * GLA_DK
    oq, ok, ov = OFF["d_q"] // hk, OFF["d_k"] // hk, OFF["d_v"] // W
    w2p = jnp.zeros((N_DIR, ZS_W, hk), F32)
    for d in range(N_DIR):
        w2p = w2p.at[d, N_IF + d * GLA_RANK:N_IF + (d + 1) * GLA_RANK, :].set(w2[d].astype(F32))
    b2r = b2.astype(F32).reshape(N_DIR, 1, hk)
    jm = jnp.asarray(np.eye(CHUNK, dtype=np.float32)[::-1].copy(), BF16)

    def _ch(i, dirn):
        return i if dirn == 0 else nc - 1 - i

    def blk(width, off, dirn):
        return pl.BlockSpec((1, CHUNK, width), lambda bi, i: (bi, _ch(i, dirn), off))

    def full(shape):
        return pl.BlockSpec(shape, lambda bi, i: (0,) * len(shape))

    s_shape = (N_DIR, GLA_HEADS, GLA_DV, GLA_DK)
    s_spec = pl.BlockSpec((1,) + s_shape, lambda bi, i: (bi, 0, 0, 0, 0))
    in_specs, args = [], []
    for dirn in range(N_DIR):
        in_specs += [blk(hk, oq, dirn), blk(hk, ok, dirn), blk(W, ov, dirn), blk(ZS_W, 0, dirn)]
        args += [zb, zb, zb, zs]
    in_specs += [full((N_DIR, ZS_W, hk)), full((N_DIR, 1, hk)), full((CHUNK, CHUNK)),
                 full((len(GLA_LEVELS) + 1, CHUNK, CHUNK)), s_spec]
    args += [w2p, b2r, jm, _gla_masks(), state]
    o_shape = jax.ShapeDtypeStruct((b, l, W), BF16)
    out_f, out_b, s_n = pl.pallas_call(
        functools.partial(_gla_body, nc, with_out),
        grid=(b, nc),
        in_specs=in_specs,
        out_specs=[blk(W, 0, 0), blk(W, 0, 1), s_spec],
        out_shape=[o_shape, o_shape, jax.ShapeDtypeStruct((b,) + s_shape, F32)],
        scratch_shapes=[pltpu.VMEM(s_shape, F32)],
        compiler_params=_cparams(2),
        name="gla_scan",
    )(*args)
    return out_f, out_b, s_n


def _matmul_body(a_ref, b_ref, o_ref):
    o_ref[...] = jnp.dot(a_ref[...], b_ref[...].astype(BF16), preferred_element_type=F32).astype(o_ref.dtype)


def _pick_tile(n, cands):
    for c in cands:
        if n % c == 0:
            return c
    raise ValueError(f"no tile for {n}")


def pallas_matmul(a, b, out_dtype=F32, cast_b_in_kernel=False):
    m, k = a.shape
    k2, n = b.shape
    assert k == k2
    n_pad = -(-n // 128) * 128
    if n_pad != n:
        b = jnp.pad(b, ((0, 0), (0, n_pad - n)))
    m_pad = -(-m // 16) * 16
    if m_pad != m:
        a = jnp.pad(a, ((0, m_pad - m), (0, 0)))
    tm_cands = (1024, 512, 256, 128, 64, 32, 16) if k <= 4096 else (512, 256, 128, 64, 32, 16)
    tm = _pick_tile(m_pad, tm_cands)
    tn = _pick_tile(n_pad, (512, 256, 128))
    out = pl.pallas_call(
        _matmul_body,
        grid=(m_pad // tm, n_pad // tn),
        in_specs=[pl.BlockSpec((tm, k), lambda i, j: (i, 0)),
                  pl.BlockSpec((k, tn), lambda i, j: (0, j))],
        out_specs=pl.BlockSpec((tm, tn), lambda i, j: (i, j)),
        out_shape=jax.ShapeDtypeStruct((m_pad, n_pad), out_dtype),
        compiler_params=_cparams(2),
        name="proj_matmul",
    )(a.astype(BF16), b if cast_b_in_kernel else b.astype(BF16))
    if m_pad != m or n_pad != n:
        out = out[:m, :n]
    return out


def _bmm_body(a_ref, b_ref, o_ref):
    o_ref[0] = jnp.dot(a_ref[...], b_ref[0], preferred_element_type=F32).astype(o_ref.dtype)


def pallas_bmm(a, x, out_dtype=F32):
    m, k = a.shape
    bsz, k2, n = x.shape
    assert k == k2
    tm = _pick_tile(m, (512, 256, 128))
    tn = _pick_tile(n, (512, 256, 128))
    return pl.pallas_call(
        _bmm_body,
        grid=(bsz, m // tm, n // tn),
        in_specs=[pl.BlockSpec((tm, k), lambda bi, i, j: (i, 0)),
                  pl.BlockSpec((1, k, tn), lambda bi, i, j: (bi, 0, j))],
        out_specs=pl.BlockSpec((1, tm, tn), lambda bi, i, j: (bi, i, j)),
        out_shape=jax.ShapeDtypeStruct((bsz, m, n), out_dtype),
        compiler_params=_cparams(3),
        name="dft_matmul",
    )(a.astype(BF16), x.astype(BF16))


def mm(a, b, out_dtype=F32):
    lead = a.shape[:-1]
    out = pallas_matmul(a.reshape(-1, a.shape[-1]), b, out_dtype)
    return out.reshape(*lead, b.shape[-1])


def rmsnorm(x, g):
    xf = x.astype(F32)
    y = xf * lax.rsqrt(jnp.mean(xf * xf, axis=-1, keepdims=True) + EPS)
    return (y * g.astype(F32)).astype(x.dtype)


def repack_w_in(w_in):
    wbig = jnp.concatenate([w_in[:, :ORIG_IF], w_in[:, ORIG_IF + N_IF:ORIG_DA]], axis=1)
    wsmall = jnp.concatenate([w_in[:, ORIG_IF:ORIG_IF + N_IF], w_in[:, ORIG_DA:ORIG_DA + N_DA]], axis=1)
    wsmall = jnp.pad(wsmall, ((0, 0), (0, ZS_W - N_IF - N_DA)))
    return wbig.astype(BF16), wsmall.astype(BF16)


def short_conv(x, w, b):
    xp = jnp.pad(x, ((0, 0), (1, 1), (0, 0)))
    return xp[:, :-2] * w[0] + xp[:, 1:-1] * w[1] + xp[:, 2:] * w[2] + b


DFT_BLK = 64


def dft_cos_sin(n_rows, n_cols, modulus):
    s = jnp.arange(n_cols, dtype=jnp.int32)[None, :]
    f0 = jnp.arange(DFT_BLK, dtype=jnp.int32)[:, None]
    f1 = jnp.arange(n_rows // DFT_BLK, dtype=jnp.int32)[:, None] * DFT_BLK

    def tab(f):
        ang = ((f * s) % modulus).astype(F32) * (2.0 * math.pi / modulus)
        return jnp.cos(ang), jnp.sin(ang)
    ca, sa = tab(f1)
    cb, sb = tab(f0)
    c = ca[:, None, :] * cb[None] - sa[:, None, :] * sb[None]
    sn = sa[:, None, :] * cb[None] + ca[:, None, :] * sb[None]
    return c.reshape(n_rows, n_cols), sn.reshape(n_rows, n_cols)


def fourier_mats(l):
    c, sn = dft_cos_sin(l, l, l)
    gf = jnp.concatenate([c, -sn], axis=1) * ((l * FNET_GW) ** -0.5)
    cg, sg = dft_cos_sin(FNET_GW, FNET_GW, FNET_GW)
    eye = jnp.eye(FNET_GROUPS, dtype=F32)
    bd = jnp.concatenate([jnp.kron(eye, cg), jnp.kron(eye, sg)], axis=1)
    return gf.astype(BF16), bd.astype(BF16)


def fourier_mix(u, w, mats):
    gf, bd = mats
    b, l, _ = u.shape
    p = mm(u, bd, BF16)
    pq = jnp.concatenate([p[..., :W], p[..., W:]], axis=1)
    return mm(pallas_bmm(gf, pq, BF16), w)


def hyena_mats(l):
    c, sn = dft_cos_sin(l, l, 2 * l)
    alt = jnp.where(jnp.arange(l) % 2 == 0, 1.0, -1.0).astype(F32)
    g = jnp.concatenate([c, sn.at[0].set(alt)], axis=0)
    return g.astype(BF16), g.T.astype(BF16), alt


def hyena_filter_coeffs(l, p, mats):
    g, _, alt = mats
    t = jnp.linspace(0.0, 1.0, l, dtype=F32)[:, None]
    bands = jnp.linspace(1e-4, HYENA_BANDS - 1, HYENA_BANDS, dtype=F32)[None, :]
    ang = (2.0 * math.pi / l) * jnp.arange(l, dtype=F32)[:, None] * bands
    feats = jnp.concatenate([t, jnp.cos(ang), -jnp.sin(ang)], axis=-1)
    fr = p["hy_f_freq"].astype(F32)
    hid = jnp.sin(fr[0] * (feats @ p["hy_f_w1"].astype(F32) + p["hy_f_b1"].astype(F32)))
    hid = jnp.sin(fr[1] * (hid @ p["hy_f_w2"].astype(F32) + p["hy_f_b2"].astype(F32)))
    filt = (hid @ p["hy_f_w3"].astype(F32)).reshape(l, HYENA_ORDER, N_DIR, GROUP_W)
    deltas = jnp.abs(jnp.linspace(HYENA_SLOW_DECAY, HYENA_FAST_DECAY, GROUP_W, dtype=F32))
    filt = filt * jnp.exp(-t * deltas)[:, None, None, :]
    fwd, bwd = filt[:, :, 0], filt[:, :, 1]
    bwd = bwd.at[0].set(0.0)
    norm = jnp.sum(jnp.abs(fwd), axis=0, keepdims=True) + jnp.sum(jnp.abs(bwd), axis=0, keepdims=True) + EPS
    n_col = HYENA_ORDER * GROUP_W
    fp = ((fwd + bwd) / norm).reshape(l, n_col)
    fm = ((fwd - bwd) / norm).reshape(l, n_col)
    kre = pallas_matmul(g[:l], fp)
    kim = -pallas_matmul(g[l:], fm)
    k_nyq = jnp.sum(alt[:, None] * fp, axis=0)
    row0 = (jnp.arange(l) == 0)[:, None]
    kim = jnp.where(row0, 0.0, kim)
    a4 = jnp.where(row0, k_nyq[None, :], kre)
    coef = jnp.where(row0, 0.5 / l, 1.0 / l)
    return tuple((coef * a).reshape(l, HYENA_ORDER, GROUP_W) for a in (kre, kim, a4))


def long_conv(z, mats, ck):
    g, gt, _ = mats
    l = z.shape[1]
    zhat = pallas_bmm(g, z)
    zc, zs = zhat[:, :l], zhat[:, l:]
    kre, kim, a4 = ck
    y = jnp.concatenate([zc * kre + zs * kim, zs * a4 - zc * kim], axis=1)
    return pallas_bmm(gt, y)


def hyena_mix(u, conv_w, conv_b, coeffs, bias, mats):
    u = short_conv(u, conv_w, conv_b).astype(F32)
    v, x1, x2 = jnp.split(u, 3, axis=-1)
    bias = bias.astype(F32)
    z = x1 * (long_conv(v, mats, tuple(a[:, 0] for a in coeffs)) + bias[0] * v)
    return x2 * (long_conv(z, mats, tuple(a[:, 1] for a in coeffs)) + bias[1] * z)


def gla_out(o, g):
    b, l, _ = o.shape
    oh = o.reshape(b, l, GLA_HEADS, GLA_DV)
    oh = oh * lax.rsqrt(jnp.mean(oh * oh, axis=-1, keepdims=True) + EPS)
    return oh.reshape(b, l, GROUP_W) * g.astype(F32)


def zcol(zb, name, width):
    return zb[..., OFF[name]:OFF[name] + width].astype(F32)


def combine(zb, o_m, o_g, p, mats):
    l = zb.shape[1]
    f_mats, h_mats = mats

    def silu(a):
        return jax.nn.silu(a)
    y_a = fourier_mix(zcol(zb, "a_u", W), p["fnet_w"], f_mats) * silu(zcol(zb, "a_g", W))
    y_b = hyena_mix(zcol(zb, "b_u", 3 * W), p["hy_conv_w"], p["hy_conv_b"], hyena_filter_coeffs(l, p, h_mats),
                    p["hy_bias"], h_mats) * silu(zcol(zb, "b_g", W))
    y_c = o_m * jax.nn.sigmoid(zcol(zb, "c_o", W)) * silu(zcol(zb, "c_g", W))
    y_d = gla_out(o_g, p["gla_norm_g"]) * silu(zcol(zb, "d_g", W))
    y = jnp.concatenate([y_a, y_b, y_c, y_d], axis=-1)
    return mm(y, p["w_out"])


def mixer_layer(x, xc, c, c_ctx, p, last, mats, mats_c):
    b, l, d = x.shape
    mod3 = pallas_matmul(jnp.concatenate([jax.nn.silu(c), jax.nn.silu(c_ctx)[None]], axis=0), p["ada_w"],
                         cast_b_in_kernel=True) + p["ada_b"]
    shift, scale, gate = jnp.split(mod3[:b], 3, axis=-1)
    mod_c = mod3[b]
    h = rmsnorm(x, p["norm_g"]) * (1 + scale[:, None]) + shift[:, None]
    hc = rmsnorm(xc, p["norm_g"]) * (1 + mod_c[d:2 * d]) + mod_c[:d]
    wbig, wsmall = repack_w_in(p["w_in"])
    zb = mm(h, wbig, BF16)
    zs = mm(h, wsmall)
    zcb = mm(hc, wbig, BF16)
    zcs = mm(hc, wsmall)

    cmf, cmb, m_states = mlstm_scan(zcb, zcs, p["ml_conv_w"], p["ml_conv_b"], p["ml_gate_b"],
                                    mlstm_zero_state(b), col_major=False)
    lmf, lmb, _ = mlstm_scan(zb, zs, p["ml_conv_w"], p["ml_conv_b"], p["ml_gate_b"], m_states, col_major=True)
    lat_m = lmf.astype(F32) + lmb.astype(F32)

    cgf, cgb, g_states = gla_scan(zcb, zcs, p["gla_w2"], p["gla_b2"], gla_zero_state(b))
    lgf, lgb, _ = gla_scan(zb, zs, p["gla_w2"], p["gla_b2"], g_states)
    lat_g = lgf.astype(F32) + lgb.astype(F32)

    x = x + gate[:, None] * combine(zb, lat_m, lat_g, p, mats)
    if not last:
        ctx_m = cmf.astype(F32) + cmb.astype(F32)
        ctx_g = cgf.astype(F32) + cgb.astype(F32)
        xc = xc + mod_c[2 * d:] * combine(zcb, ctx_m, ctx_g, p, mats_c)
    return x, xc


def kernel(x, c, ctx, c_ctx, ada_w, ada_b, norm_g, w_in, fnet_w, hy_conv_w, hy_conv_b, hy_f_w1, hy_f_b1,
           hy_f_w2, hy_f_b2, hy_f_w3, hy_f_freq, hy_bias, ml_conv_w, ml_conv_b, ml_gate_b, gla_w2, gla_b2,
           gla_norm_g, w_out, final_g):
    xc = ctx
    depth = ada_w.shape[0]
    mats = (fourier_mats(x.shape[1]), hyena_mats(x.shape[1]))
    mats_c = (fourier_mats(ctx.shape[1]), hyena_mats(ctx.shape[1]))
    for i in range(depth):
        p = {
            "ada_w": ada_w[i], "ada_b": ada_b[i], "norm_g": norm_g[i], "w_in": w_in[i], "fnet_w": fnet_w[i],
            "hy_conv_w": hy_conv_w[i], "hy_conv_b": hy_conv_b[i], "hy_f_w1": hy_f_w1[i], "hy_f_b1": hy_f_b1[i],
            "hy_f_w2": hy_f_w2[i], "hy_f_b2": hy_f_b2[i], "hy_f_w3": hy_f_w3[i], "hy_f_freq": hy_f_freq[i],
            "hy_bias": hy_bias[i], "ml_conv_w": ml_conv_w[i], "ml_conv_b": ml_conv_b[i], "ml_gate_b": ml_gate_b[i],
            "gla_w2": gla_w2[i], "gla_b2": gla_b2[i], "gla_norm_g": gla_norm_g[i], "w_out": w_out[i],
        }
        x, xc = mixer_layer(x, xc, c, c_ctx, p, i == depth - 1, mats, mats_c)
    return rmsnorm(x, final_g)
```

```python
import functools
import math

import numpy as np
import jax
import jax.numpy as jnp
from jax import lax
from jax.experimental import pallas as pl
from jax.experimental.pallas import tpu as pltpu

D_MODEL = 4096
GRID_W = 64
N_GROUPS = 4
GROUP_W = D_MODEL // N_GROUPS
FNET_GROUPS = 4
FNET_GW = GROUP_W // FNET_GROUPS
HYENA_ORDER = 2
HYENA_BANDS = 16
HYENA_FAST_DECAY = math.log(1e-2) / 0.3
HYENA_SLOW_DECAY = math.log(1e-2) / 1.5
MLSTM_HEADS = 4
MLSTM_DH = GROUP_W // MLSTM_HEADS
GLA_HEADS = 4
GLA_DV = GROUP_W // GLA_HEADS
GLA_DK = GLA_DV // 2
GLA_RANK = 16
GLA_TAU = 16.0
N_DIR = 2
CHUNK = 64
EPS = 1e-6
F32 = jnp.float32
BF16 = jnp.bfloat16

W = GROUP_W
OFF = {"a_u": 0, "a_g": W, "b_u": 2 * W, "b_g": 5 * W, "c_q": 6 * W, "c_k": 7 * W, "c_v": 8 * W,
       "c_o": 9 * W, "c_g": 10 * W, "d_q": 11 * W, "d_k": 11 * W + 512, "d_v": 12 * W, "d_g": 13 * W}
ZB_W = 14 * W
ZS_W = 128
N_IF = N_DIR * 2 * MLSTM_HEADS
N_DA = N_DIR * GLA_RANK
ORIG_IF = 11 * W
ORIG_DA = ORIG_IF + N_IF + 2 * 512 + 2 * W

V7X_VMEM_LIMIT_BYTES = 56 * 1024 * 1024
SUBLANES = 8


def _cparams(n_axes):
    return pltpu.CompilerParams(dimension_semantics=("arbitrary",) * n_axes,
                                vmem_limit_bytes=V7X_VMEM_LIMIT_BYTES)


def _split3(a):
    hi = a.astype(BF16)
    r1 = a - hi.astype(F32)
    mid = r1.astype(BF16)
    lo = (r1 - mid.astype(F32)).astype(BF16)
    return hi, mid, lo


def _dot(a, b):
    return jnp.dot(a, b, preferred_element_type=F32)


def _dot_nt(a, b):
    return lax.dot_general(a, b, (((1,), (1,)), ((), ())), preferred_element_type=F32)


def _dot_tn(a, b):
    return lax.dot_general(a, b, (((0,), (0,)), ((), ())), preferred_element_type=F32)


def _dot_exact_lhs(m_bf16, x):
    hi, mid, lo = _split3(x)
    return _dot(m_bf16, hi) + _dot(m_bf16, mid) + _dot(m_bf16, lo)


def _dot_exact_rhs(x, m_bf16):
    hi, mid, lo = _split3(x)
    return _dot(hi, m_bf16) + _dot(mid, m_bf16) + _dot(lo, m_bf16)


def _dot_f32x3(a, b):
    ah = a.astype(BF16)
    al = (a - ah.astype(F32)).astype(BF16)
    bh = b.astype(BF16)
    bl = (b - bh.astype(F32)).astype(BF16)
    return _dot(ah, bh) + _dot(ah, bl) + _dot(al, bh)


def _log_sigmoid(x):
    return jnp.minimum(x, 0.0) - jnp.log1p(jnp.exp(-jnp.abs(x)))


def _silu(x):
    return x * jax.nn.sigmoid(x)


def _row_iota(shape):
    return lax.broadcasted_iota(jnp.int32, shape, 0)


def _shift_down(x, d):
    n = x.shape[0]
    if d % SUBLANES == 0:
        return jnp.concatenate([jnp.zeros((d,) + x.shape[1:], x.dtype), x[: n - d]], axis=0)
    return jnp.where(_row_iota(x.shape) >= d, pltpu.roll(x, d, 0), 0.0)


def _shift_up(x, d):
    n = x.shape[0]
    if d == 0:
        return x
    if d % SUBLANES == 0:
        return jnp.concatenate([x[d:], jnp.zeros((d,) + x.shape[1:], x.dtype)], axis=0)
    return jnp.where(_row_iota(x.shape) < n - d, pltpu.roll(x, n - d, 0), 0.0)


def _conv3_silu(x_ref, prev_ref, next_ref, w_ref, tap0, bias_row, is_first, is_last):
    x = x_ref[0].astype(F32)
    n = x.shape[0]
    rows = _row_iota(x.shape)
    prev_row = jnp.where(is_first, 0.0, prev_ref[0][SUBLANES - 1:SUBLANES, :].astype(F32))
    next_row = jnp.where(is_last, 0.0, next_ref[0][0:1, :].astype(F32))
    xm = jnp.where(rows == 0, prev_row, pltpu.roll(x, 1, 0))
    xp = jnp.where(rows == n - 1, next_row, pltpu.roll(x, n - 1, 0))
    y = (xm * w_ref[tap0:tap0 + 1, :] + x * w_ref[tap0 + 1:tap0 + 2, :] + xp * w_ref[tap0 + 2:tap0 + 3, :]
         + w_ref[bias_row:bias_row + 1, :])
    return _silu(y)


def _mlstm_body(nc, *refs):
    (qf, kf, vf, qfp, qfn, kfp, kfn, gcf,
     qb, kb, vb, qbp, qbn, kbp, kbn, gcb,
     cw_ref, gbc_ref, tri_ref, c0_ref, n0_ref, m0_ref,
     of_ref, ob_ref, cN_ref, nN_ref, mN_ref, c_sc, n_sc, m_sc) = refs
    i = pl.program_id(1)

    @pl.when(i == 0)
    def _():
        c_sc[...] = c0_ref[0]
        n_sc[...] = n0_ref[0]
        m_sc[...] = m0_ref[0]

    dh = MLSTM_DH
    for d in range(N_DIR):
        q_ref, k_ref, v_ref, qp, qn, kp, kn, gc_ref, o_ref = (
            (qf, kf, vf, qfp, qfn, kfp, kfn, gcf, of_ref) if d == 0 else
            (qb, kb, vb, qbp, qbn, kbp, kbn, gcb, ob_ref))
        chunk = i if d == 0 else nc - 1 - i
        is_first, is_last = chunk == 0, chunk == nc - 1
        q = _conv3_silu(q_ref, qp, qn, cw_ref, 0, 6, is_first, is_last)
        k = _conv3_silu(k_ref, kp, kn, cw_ref, 3, 7, is_first, is_last) * (dh ** -0.5)
        v = v_ref[0]
        tri = tri_ref[d]
        tri_t = tri_ref[1 - d]
        gcol = gc_ref[0] + gbc_ref[...]
        g_hi, g_mid, g_lo = _split3(gcol)
        eye = tri_ref[2]
        grow = (_dot_tn(g_hi, eye) + _dot_tn(g_mid, eye) + _dot_tn(g_lo, eye))[:N_IF, :]
        bcum_c = _dot_exact_lhs(tri, _log_sigmoid(gcol))
        bcum_r = _dot_exact_rhs(_log_sigmoid(grow), tri_t)
        mask = tri.astype(F32) > 0.5
        last = CHUNK - 1 if d == 0 else 0
        q16, k16 = q.astype(BF16), k.astype(BF16)
        outs = []
        for h in range(MLSTM_HEADS):
            ci, cf = d * 8 + h, d * 8 + MLSTM_HEADS + h
            bc = bcum_c[:, cf:cf + 1]
            li_c = gcol[:, ci:ci + 1]
            br = bcum_r[cf:cf + 1, :]
            li_r = grow[ci:ci + 1, :]
            m_st = m_sc[d, h][0:1, 0:1]
            logw = jnp.where(mask, bc - (br - li_r), -jnp.inf)
            m_pos = jnp.maximum(bc + m_st, jnp.max(logw, axis=1, keepdims=True))
            w_intra = jnp.exp(logw - m_pos)
            w_prev = jnp.exp(bc + m_st - m_pos)
            sl = slice(h * dh, (h + 1) * dh)
            qh, kh, vh = q16[:, sl], k16[:, sl], v[:, sl]
            c_st = c_sc[d, h]
            n_st = n_sc[d, h][0:1, :]
            s = _dot_nt(qh, kh) * w_intra
            num = _dot(s.astype(BF16), vh) + w_prev * _dot(qh, c_st.astype(BF16))
            den = jnp.sum(s, axis=1, keepdims=True) + w_prev * jnp.sum(q[:, sl] * n_st, axis=1, keepdims=True)
            outs.append(num / jnp.maximum(jnp.abs(den), jnp.exp(-m_pos)))
            m_next = m_pos[last:last + 1, :]
            bc_last = bc[last:last + 1, :]
            w_state = jnp.exp(bc_last - bc + li_c - m_next)
            decay = jnp.exp(bc_last + m_st - m_next)
            kw = k[:, sl] * w_state
            c_sc[d, h] = decay * c_st + _dot_tn(kw.astype(BF16), vh)
            n_sc[d, h] = jnp.broadcast_to(decay * n_st + jnp.sum(kw, axis=0, keepdims=True), (SUBLANES, dh))
            m_sc[d, h] = jnp.broadcast_to(m_next, (SUBLANES, 128))
        o_ref[0] = jnp.concatenate(outs, axis=1).astype(o_ref.dtype)

    @pl.when(i == nc - 1)
    def _():
        cN_ref[0] = c_sc[...]
        nN_ref[0] = n_sc[...]
        mN_ref[0] = m_sc[...]


def _tri_consts():
    t = np.tril(np.ones((CHUNK, CHUNK), np.float32))
    return jnp.asarray(np.stack([t, t.T, np.eye(CHUNK, dtype=np.float32)]), BF16)


def mlstm_zero_state(b):
    return (jnp.zeros((b, N_DIR, MLSTM_HEADS, MLSTM_DH, MLSTM_DH), F32),
            jnp.zeros((b, N_DIR, MLSTM_HEADS, SUBLANES, MLSTM_DH), F32),
            jnp.zeros((b, N_DIR, MLSTM_HEADS, SUBLANES, 128), F32))


def mlstm_scan(zb, zs, conv_w, conv_b, gate_b, state, col_major):
    b, l, _ = zb.shape
    nc = l // CHUNK
    nblk = ZB_W // W
    oq, ok, ov = OFF["c_q"] // W, OFF["c_k"] // W, OFF["c_v"] // W
    cw8 = jnp.concatenate([conv_w.reshape(6, W), conv_b.reshape(2, W)], axis=0).astype(F32)
    gb = gate_b.astype(F32).reshape(N_IF)
    gbc = jnp.zeros((1, ZS_W), F32).at[0, :N_IF].set(gb)
    if col_major:
        rows = l // GRID_W
        assert rows == CHUNK and GRID_W == nc
        assert ok == oq + 1 and ov == oq + 2
        nblk = 3
        zb_v = zb[..., oq * W:(oq + nblk) * W].reshape(b, rows, GRID_W * nblk * W)
        oq, ok, ov = 0, 1, 2
        zs_v = zs.reshape(b, rows, GRID_W * ZS_W)
        last_rb = rows // SUBLANES - 1

        def main(off, dirn):
            return pl.BlockSpec((1, CHUNK, W), lambda bi, i: (bi, 0, _ch(i, dirn) * nblk + off))

        def prev(off, dirn):
            return pl.BlockSpec((1, SUBLANES, W),
                                lambda bi, i: (bi, last_rb, jnp.maximum(_ch(i, dirn) - 1, 0) * nblk + off))

        def nxt(off, dirn):
            return pl.BlockSpec((1, SUBLANES, W),
                                lambda bi, i: (bi, 0, jnp.minimum(_ch(i, dirn) + 1, nc - 1) * nblk + off))

        def gcol(dirn):
            return pl.BlockSpec((1, CHUNK, ZS_W), lambda bi, i: (bi, 0, _ch(i, dirn)))

        def out_spec(dirn):
            return pl.BlockSpec((1, CHUNK, W), lambda bi, i: (bi, 0, _ch(i, dirn)))
        out_shape_o = jax.ShapeDtypeStruct((b, rows, GRID_W * W), BF16)
    else:
        zb_v, zs_v = zb, zs
        cpb = CHUNK // SUBLANES

        def main(off, dirn):
            return pl.BlockSpec((1, CHUNK, W), lambda bi, i: (bi, _ch(i, dirn), off))

        def prev(off, dirn):
            return pl.BlockSpec((1, SUBLANES, W),
                                lambda bi, i: (bi, jnp.maximum(_ch(i, dirn) * cpb - 1, 0), off))

        def nxt(off, dirn):
            return pl.BlockSpec((1, SUBLANES, W),
                                lambda bi, i: (bi, jnp.minimum((_ch(i, dirn) + 1) * cpb, nc * cpb - 1), off))

        def gcol(dirn):
            return pl.BlockSpec((1, CHUNK, ZS_W), lambda bi, i: (bi, _ch(i, dirn), 0))

        def out_spec(dirn):
            return pl.BlockSpec((1, CHUNK, W), lambda bi, i: (bi, _ch(i, dirn), 0))
        out_shape_o = jax.ShapeDtypeStruct((b, l, W), BF16)

    def _ch(i, dirn):
        return i if dirn == 0 else nc - 1 - i

    def full(shape):
        return pl.BlockSpec(shape, lambda bi, i: (0,) * len(shape))

    def per_b(shape):
        return pl.BlockSpec((1,) + shape, lambda bi, i: (bi,) + (0,) * len(shape))

    in_specs, args = [], []
    for dirn in range(N_DIR):
        in_specs += [main(oq, dirn), main(ok, dirn), main(ov, dirn), prev(oq, dirn), nxt(oq, dirn),
                     prev(ok, dirn), nxt(ok, dirn), gcol(dirn)]
        args += [zb_v, zb_v, zb_v, zb_v, zb_v, zb_v, zb_v, zs_v]
    c_shape = (N_DIR, MLSTM_HEADS, MLSTM_DH, MLSTM_DH)
    n_shape = (N_DIR, MLSTM_HEADS, SUBLANES, MLSTM_DH)
    m_shape = (N_DIR, MLSTM_HEADS, SUBLANES, 128)
    in_specs += [full((8, W)), full((1, ZS_W)), full((3, CHUNK, CHUNK)),
                 per_b(c_shape), per_b(n_shape), per_b(m_shape)]
    args += [cw8, gbc, _tri_consts(), *state]
    out_f, out_b, c_n, n_n, m_n = pl.pallas_call(
        functools.partial(_mlstm_body, nc),
        grid=(b, nc),
        in_specs=in_specs,
        out_specs=[out_spec(0), out_spec(1), per_b(c_shape), per_b(n_shape), per_b(m_shape)],
        out_shape=[out_shape_o, out_shape_o,
                   jax.ShapeDtypeStruct((b,) + c_shape, F32), jax.ShapeDtypeStruct((b,) + n_shape, F32),
                   jax.ShapeDtypeStruct((b,) + m_shape, F32)],
        scratch_shapes=[pltpu.VMEM(c_shape, F32), pltpu.VMEM(n_shape, F32), pltpu.VMEM(m_shape, F32)],
        compiler_params=_cparams(2),
        name="mlstm_scan",
    )(*args)
    return out_f.reshape(b, l, W), out_b.reshape(b, l, W), (c_n, n_n, m_n)


GLA_LEVELS = (1, 2, 4, 8, 16, 32)


def _gla_masks():
    t = np.arange(CHUNK)[:, None]
    s = np.arange(CHUNK)[None, :]
    ms = [(t == s)]
    for m in GLA_LEVELS:
        ms.append((t // (2 * m) == s // (2 * m)) & (t % (2 * m) >= m) & (s % (2 * m) < m))
    return jnp.asarray(np.stack(ms).astype(np.float32))


def _cumsum_rows(x):
    d = 1
    while d < x.shape[0]:
        x = x + _shift_down(x, d)
        d *= 2
    return x


def _prev_block_end(x, m):
    n, w = x.shape
    if m >= SUBLANES:
        ends = x.reshape(n // m, m, w)[:, m - 1:m, :]
        prev = jnp.concatenate([jnp.zeros((1, 1, w), x.dtype), ends[:-1]], axis=0)
        return jnp.broadcast_to(prev, (n // m, m, w)).reshape(n, w)
    r = _row_iota(x.shape) % m
    y = _shift_down(x, 1)
    for j in range(1, m):
        y = jnp.where(r == j, _shift_down(x, j + 1), y)
    return y


def _own_block_end(x, m):
    n, w = x.shape
    if m >= SUBLANES:
        ends = x.reshape(n // m, m, w)[:, m - 1:m, :]
        return jnp.broadcast_to(ends, (n // m, m, w)).reshape(n, w)
    r = _row_iota(x.shape) % m
    y = x
    for j in range(m - 1):
        y = jnp.where(r == j, _shift_up(x, m - 1 - j), y)
    return y


def _gla_body(nc, with_out, *refs):
    (qf, kf, vf, af, qb, kb, vb, ab, w2_ref, b2_ref, j_ref, mask_ref, s0_ref,
     of_ref, ob_ref, sN_ref, s_sc) = refs
    i = pl.program_id(1)

    @pl.when(i == 0)
    def _():
        s_sc[...] = s0_ref[0]

    dk, dv = GLA_DK, GLA_DV
    for d in range(N_DIR):
        q_ref, k_ref, v_ref, a_ref, o_ref = (qf, kf, vf, af, of_ref) if d == 0 else (qb, kb, vb, ab, ob_ref)
        q16, k16, v16, a = q_ref[0], k_ref[0], v_ref[0], a_ref[0]
        if d == 1:
            jm = j_ref[...]
            q16 = _dot(jm, q16).astype(BF16)
            k16 = _dot(jm, k16).astype(BF16)
            v16 = _dot(jm, v16).astype(BF16)
            a = _dot_exact_lhs(jm, a)
        la = _log_sigmoid(_dot_f32x3(a, w2_ref[d]) + b2_ref[d]) * (1.0 / GLA_TAU)
        bc = _cumsum_rows(la)
        q = q16.astype(F32) * (dk ** -0.5)
        k = k16.astype(F32)
        b_last = bc[CHUNK - 1:CHUNK, :]
        q_in = (q * jnp.exp(bc)).astype(BF16)
        k_dec = (k * jnp.exp(b_last - bc)).astype(BF16)
        e_last = jnp.exp(b_last)
        if with_out:
            qs, ks = [q.astype(BF16)], [k16]
            for m in GLA_LEVELS:
                qs.append((q * jnp.exp(bc - _prev_block_end(bc, m))).astype(BF16))
                ks.append((k * jnp.exp(_own_block_end(bc, m) - bc)).astype(BF16))
        outs = []
        for h in range(GLA_HEADS):
            sk = slice(h * dk, (h + 1) * dk)
            sv = slice(h * dv, (h + 1) * dv)
            st = s_sc[d, h]
            if with_out:
                att = jnp.zeros((CHUNK, CHUNK), F32)
                for lvl in range(len(GLA_LEVELS) + 1):
                    att = att + mask_ref[lvl] * _dot_nt(qs[lvl][:, sk], ks[lvl][:, sk])
                outs.append(_dot(att.astype(BF16), v16[:, sv]) + _dot_nt(q_in[:, sk], st.astype(BF16)))
            s_sc[d, h] = st * e_last[:, sk] + _dot_tn(v16[:, sv], k_dec[:, sk])
        if with_out:
            o = jnp.concatenate(outs, axis=1).astype(BF16)
            if d == 1:
                o = _dot(j_ref[...], o).astype(BF16)
            o_ref[0] = o
        else:
            o_ref[0] = jnp.zeros(o_ref.shape[1:], o_ref.dtype)

    @pl.when(i == nc - 1)
    def _():
        sN_ref[0] = s_sc[...]


def gla_zero_state(b):
    return jnp.zeros((b, N_DIR, GLA_HEADS, GLA_DV, GLA_DK), F32)


def gla_scan(zb, zs, w2, b2, state, with_out=True):
    b, l, _ = zb.shape
    nc = l // CHUNK
    hk = GLA_HEADS * GLA_DK
    oq, ok, ov = OFF["d_q"] // hk, OFF["d_k"] // hk, OFF["d_v"] // W
    w2p = jnp.zeros((N_DIR, ZS_W, hk), F32)
    for d in range(N_DIR):
        w2p = w2p.at[d, N_IF + d * GLA_RANK:N_IF + (d + 1) * GLA_RANK, :].set(w2[d].astype(F32))
    b2r = b2.astype(F32).reshape(N_DIR, 1, hk)
    jm = jnp.asarray(np.eye(CHUNK, dtype=np.float32)[::-1].copy(), BF16)

    def _ch(i, dirn):
        return i if dirn == 0 else nc - 1 - i

    def blk(width, off, dirn):
        return pl.BlockSpec((1, CHUNK, width), lambda bi, i: (bi, _ch(i, dirn), off))

    def full(shape):
        return pl.BlockSpec(shape, lambda bi, i: (0,) * len(shape))

    s_shape = (N_DIR, GLA_HEADS, GLA_DV, GLA_DK)
    s_spec = pl.BlockSpec((1,) + s_shape, lambda bi, i: (bi, 0, 0, 0, 0))
    in_specs, args = [], []
    for dirn in range(N_DIR):
        in_specs += [blk(hk, oq, dirn), blk(hk, ok, dirn), blk(W, ov, dirn), blk(ZS_W, 0, dirn)]
        args += [zb, zb, zb, zs]
    in_specs += [full((N_DIR, ZS_W, hk)), full((N_DIR, 1, hk)), full((CHUNK, CHUNK)),
                 full((len(GLA_LEVELS) + 1, CHUNK, CHUNK)), s_spec]
    args += [w2p, b2r, jm, _gla_masks(), state]
    o_shape = jax.ShapeDtypeStruct((b, l, W), BF16)
    out_f, out_b, s_n = pl.pallas_call(
        functools.partial(_gla_body, nc, with_out),
        grid=(b, nc),
        in_specs=in_specs,
        out_specs=[blk(W, 0, 0), blk(W, 0, 1), s_spec],
        out_shape=[o_shape, o_shape, jax.ShapeDtypeStruct((b,) + s_shape, F32)],
        scratch_shapes=[pltpu.VMEM(s_shape, F32)],
        compiler_params=_cparams(2),
        name="gla_scan",
    )(*args)
    return out_f, out_b, s_n


def _matmul_body(a_ref, b_ref, o_ref):
    o_ref[...] = jnp.dot(a_ref[...], b_ref[...].astype(BF16), preferred_element_type=F32).astype(o_ref.dtype)


def _pick_tile(n, cands):
    for c in cands:
        if n % c == 0:
            return c
    raise ValueError(f"no tile for {n}")


def pallas_matmul(a, b, out_dtype=F32, cast_b_in_kernel=False):
    m, k = a.shape
    k2, n = b.shape
    assert k == k2
    n_pad = -(-n // 128) * 128
    if n_pad != n:
        b = jnp.pad(b, ((0, 0), (0, n_pad - n)))
    m_pad = -(-m // 16) * 16
    if m_pad != m:
        a = jnp.pad(a, ((0, m_pad - m), (0, 0)))
    tm_cands = (1024, 512, 256, 128, 64, 32, 16) if k <= 4096 else (512, 256, 128, 64, 32, 16)
    tm = _pick_tile(m_pad, tm_cands)
    tn = _pick_tile(n_pad, (512, 256, 128))
    out = pl.pallas_call(
        _matmul_body,
        grid=(m_pad // tm, n_pad // tn),
        in_specs=[pl.BlockSpec((tm, k), lambda i, j: (i, 0)),
                  pl.BlockSpec((k, tn), lambda i, j: (0, j))],
        out_specs=pl.BlockSpec((tm, tn), lambda i, j: (i, j)),
        out_shape=jax.ShapeDtypeStruct((m_pad, n_pad), out_dtype),
        compiler_params=_cparams(2),
        name="proj_matmul",
    )(a.astype(BF16), b if cast_b_in_kernel else b.astype(BF16))
    if m_pad != m or n_pad != n:
        out = out[:m, :n]
    return out


def _bmm_body(a_ref, b_ref, o_ref):
    o_ref[0] = jnp.dot(a_ref[...], b_ref[0], preferred_element_type=F32).astype(o_ref.dtype)


def pallas_bmm(a, x, out_dtype=F32):
    m, k = a.shape
    bsz, k2, n = x.shape
    assert k == k2
    tm = _pick_tile(m, (1024, 512, 256, 128) if k <= 4096 else (512, 256, 128))
    tn = _pick_tile(n, (512, 256, 128))
    return pl.pallas_call(
        _bmm_body,
        grid=(bsz, m // tm, n // tn),
        in_specs=[pl.BlockSpec((tm, k), lambda bi, i, j: (i, 0)),
                  pl.BlockSpec((1, k, tn), lambda bi, i, j: (bi, 0, j))],
        out_specs=pl.BlockSpec((1, tm, tn), lambda bi, i, j: (bi, i, j)),
        out_shape=jax.ShapeDtypeStruct((bsz, m, n), out_dtype),
        compiler_params=_cparams(3),
        name="dft_matmul",
    )(a.astype(BF16), x.astype(BF16))


def mm(a, b, out_dtype=F32):
    lead = a.shape[:-1]
    out = pallas_matmul(a.reshape(-1, a.shape[-1]), b, out_dtype)
    return out.reshape(*lead, b.shape[-1])


def rmsnorm(x, g):
    xf = x.astype(F32)
    y = xf * lax.rsqrt(jnp.mean(xf * xf, axis=-1, keepdims=True) + EPS)
    return (y * g.astype(F32)).astype(x.dtype)


def repack_w_in(w_in):
    wbig = jnp.concatenate([w_in[:, :ORIG_IF], w_in[:, ORIG_IF + N_IF:ORIG_DA]], axis=1)
    wsmall = jnp.concatenate([w_in[:, ORIG_IF:ORIG_IF + N_IF], w_in[:, ORIG_DA:ORIG_DA + N_DA]], axis=1)
    wsmall = jnp.pad(wsmall, ((0, 0), (0, ZS_W - N_IF - N_DA)))
    return wbig.astype(BF16), wsmall.astype(BF16)


def short_conv(x, w, b):
    xp = jnp.pad(x, ((0, 0), (1, 1), (0, 0)))
    return xp[:, :-2] * w[0] + xp[:, 1:-1] * w[1] + xp[:, 2:] * w[2] + b


DFT_BLK = 64


def dft_cos_sin(n_rows, n_cols, modulus):
    s = jnp.arange(n_cols, dtype=jnp.int32)[None, :]
    f0 = jnp.arange(DFT_BLK, dtype=jnp.int32)[:, None]
    f1 = jnp.arange(n_rows // DFT_BLK, dtype=jnp.int32)[:, None] * DFT_BLK

    def tab(f):
        ang = ((f * s) % modulus).astype(F32) * (2.0 * math.pi / modulus)
        return jnp.cos(ang), jnp.sin(ang)
    ca, sa = tab(f1)
    cb, sb = tab(f0)
    c = ca[:, None, :] * cb[None] - sa[:, None, :] * sb[None]
    sn = sa[:, None, :] * cb[None] + ca[:, None, :] * sb[None]
    return c.reshape(n_rows, n_cols), sn.reshape(n_rows, n_cols)


def fourier_mats(l):
    c, sn = dft_cos_sin(l, l, l)
    gf = jnp.concatenate([c, -sn], axis=1) * ((l * FNET_GW) ** -0.5)
    cg, sg = dft_cos_sin(FNET_GW, FNET_GW, FNET_GW)
    eye = jnp.eye(FNET_GROUPS, dtype=F32)
    bd = jnp.concatenate([jnp.kron(eye, cg), jnp.kron(eye, sg)], axis=1)
    return gf.astype(BF16), bd.astype(BF16)


def fourier_mix(u, w, mats):
    gf, bd = mats
    b, l, _ = u.shape
    p = mm(u, bd, BF16)
    pq = jnp.concatenate([p[..., :W], p[..., W:]], axis=1)
    return mm(pallas_bmm(gf, pq, BF16), w)


def hyena_mats(l):
    c, sn = dft_cos_sin(l, l, 2 * l)
    alt = jnp.where(jnp.arange(l) % 2 == 0, 1.0, -1.0).astype(F32)
    g = jnp.concatenate([c, sn.at[0].set(alt)], axis=0)
    return g.astype(BF16), g.T.astype(BF16), alt


def hyena_filter_coeffs(l, p, mats):
    g, _, alt = mats
    t = jnp.linspace(0.0, 1.0, l, dtype=F32)[:, None]
    bands = jnp.linspace(1e-4, HYENA_BANDS - 1, HYENA_BANDS, dtype=F32)[None, :]
    ang = (2.0 * math.pi / l) * jnp.arange(l, dtype=F32)[:, None] * bands
    feats = jnp.concatenate([t, jnp.cos(ang), -jnp.sin(ang)], axis=-1)
    fr = p["hy_f_freq"].astype(F32)
    hid = jnp.sin(fr[0] * (feats @ p["hy_f_w1"].astype(F32) + p["hy_f_b1"].astype(F32)))
    hid = jnp.sin(fr[1] * (hid @ p["hy_f_w2"].astype(F32) + p["hy_f_b2"].astype(F32)))
    filt = (hid @ p["hy_f_w3"].astype(F32)).reshape(l, HYENA_ORDER, N_DIR, GROUP_W)
    deltas = jnp.abs(jnp.linspace(HYENA_SLOW_DECAY, HYENA_FAST_DECAY, GROUP_W, dtype=F32))
    filt = filt * jnp.exp(-t * deltas)[:, None, None, :]
    fwd, bwd = filt[:, :, 0], filt[:, :, 1]
    bwd = bwd.at[0].set(0.0)
    norm = jnp.sum(jnp.abs(fwd), axis=0, keepdims=True) + jnp.sum(jnp.abs(bwd), axis=0, keepdims=True) + EPS
    n_col = HYENA_ORDER * GROUP_W
    fp = ((fwd + bwd) / norm).reshape(l, n_col)
    fm = ((fwd - bwd) / norm).reshape(l, n_col)
    kre = pallas_matmul(g[:l], fp)
    kim = -pallas_matmul(g[l:], fm)
    k_nyq = jnp.sum(alt[:, None] * fp, axis=0)
    row0 = (jnp.arange(l) == 0)[:, None]
    kim = jnp.where(row0, 0.0, kim)
    a4 = jnp.where(row0, k_nyq[None, :], kre)
    coef = jnp.where(row0, 0.5 / l, 1.0 / l)
    return tuple((coef * a).reshape(l, HYENA_ORDER, GROUP_W) for a in (kre, kim, a4))


def long_conv(z, mats, ck):
    g, gt, _ = mats
    l = z.shape[1]
    zhat = pallas_bmm(g, z)
    zc, zs = zhat[:, :l], zhat[:, l:]
    kre, kim, a4 = ck
    y = jnp.concatenate([zc * kre + zs * kim, zs * a4 - zc * kim], axis=1)
    return pallas_bmm(gt, y)


def hyena_mix(u, conv_w, conv_b, coeffs, bias, mats):
    u = short_conv(u, conv_w, conv_b).astype(F32)
    v, x1, x2 = jnp.split(u, 3, axis=-1)
    bias = bias.astype(F32)
    z = x1 * (long_conv(v, mats, tuple(a[:, 0] for a in coeffs)) + bias[0] * v)
    return x2 * (long_conv(z, mats, tuple(a[:, 1] for a in coeffs)) + bias[1] * z)


def gla_out(o, g):
    b, l, _ = o.shape
    oh = o.reshape(b, l, GLA_HEADS, GLA_DV)
    oh = oh * lax.rsqrt(jnp.mean(oh * oh, axis=-1, keepdims=True) + EPS)
    return oh.reshape(b, l, GROUP_W) * g.astype(F32)


def zcol(zb, name, width):
    return zb[..., OFF[name]:OFF[name] + width].astype(F32)


def combine(zb, o_m, o_g, p, mats):
    l = zb.shape[1]
    f_mats, h_mats = mats

    def silu(a):
        return jax.nn.silu(a)
    y_a = fourier_mix(zcol(zb, "a_u", W), p["fnet_w"], f_mats) * silu(zcol(zb, "a_g", W))
    y_b = hyena_mix(zcol(zb, "b_u", 3 * W), p["hy_conv_w"], p["hy_conv_b"], hyena_filter_coeffs(l, p, h_mats),
                    p["hy_bias"], h_mats) * silu(zcol(zb, "b_g", W))
    y_c = o_m * jax.nn.sigmoid(zcol(zb, "c_o", W)) * silu(zcol(zb, "c_g", W))
    y_d = gla_out(o_g, p["gla_norm_g"]) * silu(zcol(zb, "d_g", W))
    y = jnp.concatenate([y_a, y_b, y_c, y_d], axis=-1)
    return mm(y, p["w_out"])


def mixer_layer(x, xc, c, c_ctx, p, last, mats, mats_c):
    b, l, d = x.shape
    mod3 = pallas_matmul(jnp.concatenate([jax.nn.silu(c), jax.nn.silu(c_ctx)[None]], axis=0), p["ada_w"],
                         cast_b_in_kernel=True) + p["ada_b"]
    shift, scale, gate = jnp.split(mod3[:b], 3, axis=-1)
    mod_c = mod3[b]
    h = rmsnorm(x, p["norm_g"]) * (1 + scale[:, None]) + shift[:, None]
    hc = rmsnorm(xc, p["norm_g"]) * (1 + mod_c[d:2 * d]) + mod_c[:d]
    wbig, wsmall = repack_w_in(p["w_in"])
    zb = mm(h, wbig, BF16)
    zs = mm(h, wsmall)
    zcb = mm(hc, wbig, BF16)
    zcs = mm(hc, wsmall)

    cmf, cmb, m_states = mlstm_scan(zcb, zcs, p["ml_conv_w"], p["ml_conv_b"], p["ml_gate_b"],
                                    mlstm_zero_state(b), col_major=False)
    lmf, lmb, _ = mlstm_scan(zb, zs, p["ml_conv_w"], p["ml_conv_b"], p["ml_gate_b"], m_states, col_major=True)
    lat_m = lmf.astype(F32) + lmb.astype(F32)

    cgf, cgb, g_states = gla_scan(zcb, zcs, p["gla_w2"], p["gla_b2"], gla_zero_state(b))
    lgf, lgb, _ = gla_scan(zb, zs, p["gla_w2"], p["gla_b2"], g_states)
    lat_g = lgf.astype(F32) + lgb.astype(F32)

    x = x + gate[:, None] * combine(zb, lat_m, lat_g, p, mats)
    if not last:
        ctx_m = cmf.astype(F32) + cmb.astype(F32)
        ctx_g = cgf.astype(F32) + cgb.astype(F32)
        xc = xc + mod_c[2 * d:] * combine(zcb, ctx_m, ctx_g, p, mats_c)
    return x, xc


def kernel(x, c, ctx, c_ctx, ada_w, ada_b, norm_g, w_in, fnet_w, hy_conv_w, hy_conv_b, hy_f_w1, hy_f_b1,
           hy_f_w2, hy_f_b2, hy_f_w3, hy_f_freq, hy_bias, ml_conv_w, ml_conv_b, ml_gate_b, gla_w2, gla_b2,
           gla_norm_g, w_out, final_g):
    xc = ctx
    depth = ada_w.shape[0]
    mats = (fourier_mats(x.shape[1]), hyena_mats(x.shape[1]))
    mats_c = (fourier_mats(ctx.shape[1]), hyena_mats(ctx.shape[1]))
    for i in range(depth):
        p = {
            "ada_w": ada_w[i], "ada_b": ada_b[i], "norm_g": norm_g[i], "w_in": w_in[i], "fnet_w": fnet_w[i],
            "hy_conv_w": hy_conv_w[i], "hy_conv_b": hy_conv_b[i], "hy_f_w1": hy_f_w1[i], "hy_f_b1": hy_f_b1[i],
            "hy_f_w2": hy_f_w2[i], "hy_f_b2": hy_f_b2[i], "hy_f_w3": hy_f_w3[i], "hy_f_freq": hy_f_freq[i],
            "hy_bias": hy_bias[i], "ml_conv_w": ml_conv_w[i], "ml_conv_b": ml_conv_b[i], "ml_gate_b": ml_gate_b[i],
            "gla_w2": gla_w2[i], "gla_b2": gla_b2[i], "gla_norm_g": gla_norm_g[i], "w_out": w_out[i],
        }
        x, xc = mixer_layer(x, xc, c, c_ctx, p, i == depth - 1, mats, mats_c)
    return rmsnorm(x, final_g)
```

```python
import functools
import math

import numpy as np
import jax
import jax.numpy as jnp
from jax import lax
from jax.experimental import pallas as pl
from jax.experimental.pallas import tpu as pltpu

D_MODEL = 4096
GRID_W = 64
N_GROUPS = 4
GROUP_W = D_MODEL // N_GROUPS
FNET_GROUPS = 4
FNET_GW = GROUP_W // FNET_GROUPS
HYENA_ORDER = 2
HYENA_BANDS = 16
HYENA_FAST_DECAY = math.log(1e-2) / 0.3
HYENA_SLOW_DECAY = math.log(1e-2) / 1.5
MLSTM_HEADS = 4
MLSTM_DH = GROUP_W // MLSTM_HEADS
GLA_HEADS = 4
GLA_DV = GROUP_W // GLA_HEADS
GLA_DK = GLA_DV // 2
GLA_RANK = 16
GLA_TAU = 16.0
N_DIR = 2
CHUNK = 64
EPS = 1e-6
F32 = jnp.float32
BF16 = jnp.bfloat16

W = GROUP_W
OFF = {"a_u": 0, "a_g": W, "b_u": 2 * W, "b_g": 5 * W, "c_q": 6 * W, "c_k": 7 * W, "c_v": 8 * W,
       "c_o": 9 * W, "c_g": 10 * W, "d_q": 11 * W, "d_k": 11 * W + 512, "d_v": 12 * W, "d_g": 13 * W}
ZB_W = 14 * W
ZS_W = 128
N_IF = N_DIR * 2 * MLSTM_HEADS
N_DA = N_DIR * GLA_RANK
ORIG_IF = 11 * W
ORIG_DA = ORIG_IF + N_IF + 2 * 512 + 2 * W

V7X_VMEM_LIMIT_BYTES = 56 * 1024 * 1024
SUBLANES = 8


def _cparams(n_axes):
    return pltpu.CompilerParams(dimension_semantics=("arbitrary",) * n_axes,
                                vmem_limit_bytes=V7X_VMEM_LIMIT_BYTES)


def _split3(a):
    hi = a.astype(BF16)
    r1 = a - hi.astype(F32)
    mid = r1.astype(BF16)
    lo = (r1 - mid.astype(F32)).astype(BF16)
    return hi, mid, lo


def _dot(a, b):
    return jnp.dot(a, b, preferred_element_type=F32)


def _dot_nt(a, b):
    return lax.dot_general(a, b, (((1,), (1,)), ((), ())), preferred_element_type=F32)


def _dot_tn(a, b):
    return lax.dot_general(a, b, (((0,), (0,)), ((), ())), preferred_element_type=F32)


def _dot_exact_lhs(m_bf16, x):
    hi, mid, lo = _split3(x)
    return _dot(m_bf16, hi) + _dot(m_bf16, mid) + _dot(m_bf16, lo)


def _dot_exact_rhs(x, m_bf16):
    hi, mid, lo = _split3(x)
    return _dot(hi, m_bf16) + _dot(mid, m_bf16) + _dot(lo, m_bf16)


def _dot_f32x3(a, b):
    ah = a.astype(BF16)
    al = (a - ah.astype(F32)).astype(BF16)
    bh = b.astype(BF16)
    bl = (b - bh.astype(F32)).astype(BF16)
    return _dot(ah, bh) + _dot(ah, bl) + _dot(al, bh)


def _log_sigmoid(x):
    return jnp.minimum(x, 0.0) - jnp.log1p(jnp.exp(-jnp.abs(x)))


def _silu(x):
    return x * jax.nn.sigmoid(x)


def _row_iota(shape):
    return lax.broadcasted_iota(jnp.int32, shape, 0)


def _shift_down(x, d):
    n = x.shape[0]
    if d % SUBLANES == 0:
        return jnp.concatenate([jnp.zeros((d,) + x.shape[1:], x.dtype), x[: n - d]], axis=0)
    return jnp.where(_row_iota(x.shape) >= d, pltpu.roll(x, d, 0), 0.0)


def _shift_up(x, d):
    n = x.shape[0]
    if d == 0:
        return x
    if d % SUBLANES == 0:
        return jnp.concatenate([x[d:], jnp.zeros((d,) + x.shape[1:], x.dtype)], axis=0)
    return jnp.where(_row_iota(x.shape) < n - d, pltpu.roll(x, n - d, 0), 0.0)


def _conv3_silu(x_ref, prev_ref, next_ref, w_ref, tap0, bias_row, is_first, is_last):
    x = x_ref[0].astype(F32)
    n = x.shape[0]
    rows = _row_iota(x.shape)
    prev_row = jnp.where(is_first, 0.0, prev_ref[0][SUBLANES - 1:SUBLANES, :].astype(F32))
    next_row = jnp.where(is_last, 0.0, next_ref[0][0:1, :].astype(F32))
    xm = jnp.where(rows == 0, prev_row, pltpu.roll(x, 1, 0))
    xp = jnp.where(rows == n - 1, next_row, pltpu.roll(x, n - 1, 0))
    y = (xm * w_ref[tap0:tap0 + 1, :] + x * w_ref[tap0 + 1:tap0 + 2, :] + xp * w_ref[tap0 + 2:tap0 + 3, :]
         + w_ref[bias_row:bias_row + 1, :])
    return _silu(y)


def _mlstm_body(nc, *refs):
    (qf, kf, vf, qfp, qfn, kfp, kfn, gcf,
     qb, kb, vb, qbp, qbn, kbp, kbn, gcb,
     cw_ref, gbc_ref, tri_ref, c0_ref, n0_ref, m0_ref,
     of_ref, ob_ref, cN_ref, nN_ref, mN_ref, c_sc, n_sc, m_sc) = refs
    i = pl.program_id(1)

    @pl.when(i == 0)
    def _():
        c_sc[...] = c0_ref[0]
        n_sc[...] = n0_ref[0]
        m_sc[...] = m0_ref[0]

    dh = MLSTM_DH
    for d in range(N_DIR):
        q_ref, k_ref, v_ref, qp, qn, kp, kn, gc_ref, o_ref = (
            (qf, kf, vf, qfp, qfn, kfp, kfn, gcf, of_ref) if d == 0 else
            (qb, kb, vb, qbp, qbn, kbp, kbn, gcb, ob_ref))
        chunk = i if d == 0 else nc - 1 - i
        is_first, is_last = chunk == 0, chunk == nc - 1
        q = _conv3_silu(q_ref, qp, qn, cw_ref, 0, 6, is_first, is_last)
        k = _conv3_silu(k_ref, kp, kn, cw_ref, 3, 7, is_first, is_last) * (dh ** -0.5)
        v = v_ref[0]
        tri = tri_ref[d]
        tri_t = tri_ref[1 - d]
        gcol = gc_ref[0] + gbc_ref[...]
        g_hi, g_mid, g_lo = _split3(gcol)
        eye = tri_ref[2]
        grow = (_dot_tn(g_hi, eye) + _dot_tn(g_mid, eye) + _dot_tn(g_lo, eye))[:N_IF, :]
        bcum_c = _dot_exact_lhs(tri, _log_sigmoid(gcol))
        bcum_r = _dot_exact_rhs(_log_sigmoid(grow), tri_t)
        mask = tri.astype(F32) > 0.5
        last = CHUNK - 1 if d == 0 else 0
        q16, k16 = q.astype(BF16), k.astype(BF16)
        outs = []
        for h in range(MLSTM_HEADS):
            ci, cf = d * 8 + h, d * 8 + MLSTM_HEADS + h
            bc = bcum_c[:, cf:cf + 1]
            li_c = gcol[:, ci:ci + 1]
            br = bcum_r[cf:cf + 1, :]
            li_r = grow[ci:ci + 1, :]
            m_st = m_sc[d, h][0:1, 0:1]
            logw = jnp.where(mask, bc - (br - li_r), -jnp.inf)
            m_pos = jnp.maximum(bc + m_st, jnp.max(logw, axis=1, keepdims=True))
            w_intra = jnp.exp(logw - m_pos)
            w_prev = jnp.exp(bc + m_st - m_pos)
            sl = slice(h * dh, (h + 1) * dh)
            qh, kh, vh = q16[:, sl], k16[:, sl], v[:, sl]
            c_st = c_sc[d, h]
            n_st = n_sc[d, h][0:1, :]
            s = _dot_nt(qh, kh) * w_intra
            num = _dot(s.astype(BF16), vh) + w_prev * _dot(qh, c_st.astype(BF16))
            den = jnp.sum(s, axis=1, keepdims=True) + w_prev * jnp.sum(q[:, sl] * n_st, axis=1, keepdims=True)
            outs.append(num / jnp.maximum(jnp.abs(den), jnp.exp(-m_pos)))
            m_next = m_pos[last:last + 1, :]
            bc_last = bc[last:last + 1, :]
            w_state = jnp.exp(bc_last - bc + li_c - m_next)
            decay = jnp.exp(bc_last + m_st - m_next)
            kw = k[:, sl] * w_state
            c_sc[d, h] = decay * c_st + _dot_tn(kw.astype(BF16), vh)
            n_sc[d, h] = jnp.broadcast_to(decay * n_st + jnp.sum(kw, axis=0, keepdims=True), (SUBLANES, dh))
            m_sc[d, h] = jnp.broadcast_to(m_next, (SUBLANES, 128))
        o_ref[0] = jnp.concatenate(outs, axis=1).astype(o_ref.dtype)

    @pl.when(i == nc - 1)
    def _():
        cN_ref[0] = c_sc[...]
        nN_ref[0] = n_sc[...]
        mN_ref[0] = m_sc[...]


def _tri_consts():
    t = np.tril(np.ones((CHUNK, CHUNK), np.float32))
    return jnp.asarray(np.stack([t, t.T, np.eye(CHUNK, dtype=np.float32)]), BF16)


def mlstm_zero_state(b):
    return (jnp.zeros((b, N_DIR, MLSTM_HEADS, MLSTM_DH, MLSTM_DH), F32),
            jnp.zeros((b, N_DIR, MLSTM_HEADS, SUBLANES, MLSTM_DH), F32),
            jnp.zeros((b, N_DIR, MLSTM_HEADS, SUBLANES, 128), F32))


def mlstm_scan(zb, zs, conv_w, conv_b, gate_b, state, col_major):
    b, l, _ = zb.shape
    nc = l // CHUNK
    nblk = ZB_W // W
    oq, ok, ov = OFF["c_q"] // W, OFF["c_k"] // W, OFF["c_v"] // W
    cw8 = jnp.concatenate([conv_w.reshape(6, W), conv_b.reshape(2, W)], axis=0).astype(F32)
    gb = gate_b.astype(F32).reshape(N_IF)
    gbc = jnp.zeros((1, ZS_W), F32).at[0, :N_IF].set(gb)
    if col_major:
        rows = l // GRID_W
        assert rows == CHUNK and GRID_W == nc
        assert ok == oq + 1 and ov == oq + 2
        nblk = 3
        zb_v = zb[..., oq * W:(oq + nblk) * W].reshape(b, rows, GRID_W * nblk * W)
        oq, ok, ov = 0, 1, 2
        zs_v = zs.reshape(b, rows, GRID_W * ZS_W)
        last_rb = rows // SUBLANES - 1

        def main(off, dirn):
            return pl.BlockSpec((1, CHUNK, W), lambda bi, i: (bi, 0, _ch(i, dirn) * nblk + off))

        def prev(off, dirn):
            return pl.BlockSpec((1, SUBLANES, W),
                                lambda bi, i: (bi, last_rb, jnp.maximum(_ch(i, dirn) - 1, 0) * nblk + off))

        def nxt(off, dirn):
            return pl.BlockSpec((1, SUBLANES, W),
                                lambda bi, i: (bi, 0, jnp.minimum(_ch(i, dirn) + 1, nc - 1) * nblk + off))

        def gcol(dirn):
            return pl.BlockSpec((1, CHUNK, ZS_W), lambda bi, i: (bi, 0, _ch(i, dirn)))

        def out_spec(dirn):
            return pl.BlockSpec((1, CHUNK, W), lambda bi, i: (bi, 0, _ch(i, dirn)))
        out_shape_o = jax.ShapeDtypeStruct((b, rows, GRID_W * W), BF16)
    else:
        zb_v, zs_v = zb, zs
        cpb = CHUNK // SUBLANES

        def main(off, dirn):
            return pl.BlockSpec((1, CHUNK, W), lambda bi, i: (bi, _ch(i, dirn), off))

        def prev(off, dirn):
            return pl.BlockSpec((1, SUBLANES, W),
                                lambda bi, i: (bi, jnp.maximum(_ch(i, dirn) * cpb - 1, 0), off))

        def nxt(off, dirn):
            return pl.BlockSpec((1, SUBLANES, W),
                                lambda bi, i: (bi, jnp.minimum((_ch(i, dirn) + 1) * cpb, nc * cpb - 1), off))

        def gcol(dirn):
            return pl.BlockSpec((1, CHUNK, ZS_W), lambda bi, i: (bi, _ch(i, dirn), 0))

        def out_spec(dirn):
            return pl.BlockSpec((1, CHUNK, W), lambda bi, i: (bi, _ch(i, dirn), 0))
        out_shape_o = jax.ShapeDtypeStruct((b, l, W), BF16)

    def _ch(i, dirn):
        return i if dirn == 0 else nc - 1 - i

    def full(shape):
        return pl.BlockSpec(shape, lambda bi, i: (0,) * len(shape))

    def per_b(shape):
        return pl.BlockSpec((1,) + shape, lambda bi, i: (bi,) + (0,) * len(shape))

    in_specs, args = [], []
    for dirn in range(N_DIR):
        in_specs += [main(oq, dirn), main(ok, dirn), main(ov, dirn), prev(oq, dirn), nxt(oq, dirn),
                     prev(ok, dirn), nxt(ok, dirn), gcol(dirn)]
        args += [zb_v, zb_v, zb_v, zb_v, zb_v, zb_v, zb_v, zs_v]
    c_shape = (N_DIR, MLSTM_HEADS, MLSTM_DH, MLSTM_DH)
    n_shape = (N_DIR, MLSTM_HEADS, SUBLANES, MLSTM_DH)
    m_shape = (N_DIR, MLSTM_HEADS, SUBLANES, 128)
    in_specs += [full((8, W)), full((1, ZS_W)), full((3, CHUNK, CHUNK)),
                 per_b(c_shape), per_b(n_shape), per_b(m_shape)]
    args += [cw8, gbc, _tri_consts(), *state]
    out_f, out_b, c_n, n_n, m_n = pl.pallas_call(
        functools.partial(_mlstm_body, nc),
        grid=(b, nc),
        in_specs=in_specs,
        out_specs=[out_spec(0), out_spec(1), per_b(c_shape), per_b(n_shape), per_b(m_shape)],
        out_shape=[out_shape_o, out_shape_o,
                   jax.ShapeDtypeStruct((b,) + c_shape, F32), jax.ShapeDtypeStruct((b,) + n_shape, F32),
                   jax.ShapeDtypeStruct((b,) + m_shape, F32)],
        scratch_shapes=[pltpu.VMEM(c_shape, F32), pltpu.VMEM(n_shape, F32), pltpu.VMEM(m_shape, F32)],
        compiler_params=_cparams(2),
        name="mlstm_scan",
    )(*args)
    return out_f.reshape(b, l, W), out_b.reshape(b, l, W), (c_n, n_n, m_n)


GLA_LEVELS = (1, 2, 4, 8, 16, 32)


def _gla_masks():
    t = np.arange(CHUNK)[:, None]
    s = np.arange(CHUNK)[None, :]
    ms = [(t == s)]
    for m in GLA_LEVELS:
        ms.append((t // (2 * m) == s // (2 * m)) & (t % (2 * m) >= m) & (s % (2 * m) < m))
    return jnp.asarray(np.stack(ms).astype(np.float32))


def _cumsum_rows(x):
    d = 1
    while d < x.shape[0]:
        x = x + _shift_down(x, d)
        d *= 2
    return x


def _prev_block_end(x, m):
    n, w = x.shape
    if m >= SUBLANES:
        ends = x.reshape(n // m, m, w)[:, m - 1:m, :]
        prev = jnp.concatenate([jnp.zeros((1, 1, w), x.dtype), ends[:-1]], axis=0)
        return jnp.broadcast_to(prev, (n // m, m, w)).reshape(n, w)
    r = _row_iota(x.shape) % m
    y = _shift_down(x, 1)
    for j in range(1, m):
        y = jnp.where(r == j, _shift_down(x, j + 1), y)
    return y


def _own_block_end(x, m):
    n, w = x.shape
    if m >= SUBLANES:
        ends = x.reshape(n // m, m, w)[:, m - 1:m, :]
        return jnp.broadcast_to(ends, (n // m, m, w)).reshape(n, w)
    r = _row_iota(x.shape) % m
    y = x
    for j in range(m - 1):
        y = jnp.where(r == j, _shift_up(x, m - 1 - j), y)
    return y


def _gla_body(nc, with_out, *refs):
    (qf, kf, vf, af, qb, kb, vb, ab, w2_ref, b2_ref, j_ref, mask_ref, s0_ref,
     of_ref, ob_ref, sN_ref, s_sc) = refs
    i = pl.program_id(1)

    @pl.when(i == 0)
    def _():
        s_sc[...] = s0_ref[0]

    dk, dv = GLA_DK, GLA_DV
    for d in range(N_DIR):
        q_ref, k_ref, v_ref, a_ref, o_ref = (qf, kf, vf, af, of_ref) if d == 0 else (qb, kb, vb, ab, ob_ref)
        q16, k16, v16, a = q_ref[0], k_ref[0], v_ref[0], a_ref[0]
        if d == 1:
            jm = j_ref[...]
            q16 = _dot(jm, q16).astype(BF16)
            k16 = _dot(jm, k16).astype(BF16)
            v16 = _dot(jm, v16).astype(BF16)
            a = _dot_exact_lhs(jm, a)
        la = _log_sigmoid(_dot_f32x3(a, w2_ref[d]) + b2_ref[d]) * (1.0 / GLA_TAU)
        bc = _cumsum_rows(la)
        q = q16.astype(F32) * (dk ** -0.5)
        k = k16.astype(F32)
        b_last = bc[CHUNK - 1:CHUNK, :]
        q_in = (q * jnp.exp(bc)).astype(BF16)
        k_dec = (k * jnp.exp(b_last - bc)).astype(BF16)
        e_last = jnp.exp(b_last)
        if with_out:
            qs, ks = [q.astype(BF16)], [k16]
            for m in GLA_LEVELS:
                qs.append((q * jnp.exp(bc - _prev_block_end(bc, m))).astype(BF16))
                ks.append((k * jnp.exp(_own_block_end(bc, m) - bc)).astype(BF16))
        outs = []
        for h in range(GLA_HEADS):
            sk = slice(h * dk, (h + 1) * dk)
            sv = slice(h * dv, (h + 1) * dv)
            st = s_sc[d, h]
            if with_out:
                att = jnp.zeros((CHUNK, CHUNK), F32)
                for lvl in range(len(GLA_LEVELS) + 1):
                    att = att + mask_ref[lvl] * _dot_nt(qs[lvl][:, sk], ks[lvl][:, sk])
                outs.append(_dot(att.astype(BF16), v16[:, sv]) + _dot_nt(q_in[:, sk], st.astype(BF16)))
            s_sc[d, h] = st * e_last[:, sk] + _dot_tn(v16[:, sv], k_dec[:, sk])
        if with_out:
            o = jnp.concatenate(outs, axis=1).astype(BF16)
            if d == 1:
                o = _dot(j_ref[...], o).astype(BF16)
            o_ref[0] = o
        else:
            o_ref[0] = jnp.zeros(o_ref.shape[1:], o_ref.dtype)

    @pl.when(i == nc - 1)
    def _():
        sN_ref[0] = s_sc[...]


def gla_zero_state(b):
    return jnp.zeros((b, N_DIR, GLA_HEADS, GLA_DV, GLA_DK), F32)


def gla_scan(zb, zs, w2, b2, state, with_out=True):
    b, l, _ = zb.shape
    nc = l // CHUNK
    hk = GLA_HEADS * GLA_DK
    oq, ok, ov = OFF["d_q"] // hk, OFF["d_k"] // hk, OFF["d_v"] // W
    w2p = jnp.zeros((N_DIR, ZS_W, hk), F32)
    for d in range(N_DIR):
        w2p = w2p.at[d, N_IF + d * GLA_RANK:N_IF + (d + 1) * GLA_RANK, :].set(w2[d].astype(F32))
    b2r = b2.astype(F32).reshape(N_DIR, 1, hk)
    jm = jnp.asarray(np.eye(CHUNK, dtype=np.float32)[::-1].copy(), BF16)

    def _ch(i, dirn):
        return i if dirn == 0 else nc - 1 - i

    def blk(width, off, dirn):
        return pl.BlockSpec((1, CHUNK, width), lambda bi, i: (bi, _ch(i, dirn), off))

    def full(shape):
        return pl.BlockSpec(shape, lambda bi, i: (0,) * len(shape))

    s_shape = (N_DIR, GLA_HEADS, GLA_DV, GLA_DK)
    s_spec = pl.BlockSpec((1,) + s_shape, lambda bi, i: (bi, 0, 0, 0, 0))
    in_specs, args = [], []
    for dirn in range(N_DIR):
        in_specs += [blk(hk, oq, dirn), blk(hk, ok, dirn), blk(W, ov, dirn), blk(ZS_W, 0, dirn)]
        args += [zb, zb, zb, zs]
    in_specs += [full((N_DIR, ZS_W, hk)), full((N_DIR, 1, hk)), full((CHUNK, CHUNK)),
                 full((len(GLA_LEVELS) + 1, CHUNK, CHUNK)), s_spec]
    args += [w2p, b2r, jm, _gla_masks(), state]
    o_shape = jax.ShapeDtypeStruct((b, l, W), BF16)
    out_f, out_b, s_n = pl.pallas_call(
        functools.partial(_gla_body, nc, with_out),
        grid=(b, nc),
        in_specs=in_specs,
        out_specs=[blk(W, 0, 0), blk(W, 0, 1), s_spec],
        out_shape=[o_shape, o_shape, jax.ShapeDtypeStruct((b,) + s_shape, F32)],
        scratch_shapes=[pltpu.VMEM(s_shape, F32)],
        compiler_params=_cparams(2),
        name="gla_scan",
    )(*args)
    return out_f, out_b, s_n


def _matmul_body(a_ref, b_ref, o_ref):
    o_ref[...] = jnp.dot(a_ref[...], b_ref[...].astype(BF16), preferred_element_type=F32).astype(o_ref.dtype)


def _pick_tile(n, cands):
    for c in cands:
        if n % c == 0:
            return c
    raise ValueError(f"no tile for {n}")


def pallas_matmul(a, b, out_dtype=F32, cast_b_in_kernel=False):
    m, k = a.shape
    k2, n = b.shape
    assert k == k2
    n_pad = -(-n // 128) * 128
    if n_pad != n:
        b = jnp.pad(b, ((0, 0), (0, n_pad - n)))
    m_pad = -(-m // 16) * 16
    if m_pad != m:
        a = jnp.pad(a, ((0, m_pad - m), (0, 0)))
    tm_cands = (1024, 512, 256, 128, 64, 32, 16) if k <= 4096 else (512, 256, 128, 64, 32, 16)
    tm = _pick_tile(m_pad, tm_cands)
    tn = _pick_tile(n_pad, (512, 256, 128))
    out = pl.pallas_call(
        _matmul_body,
        grid=(m_pad // tm, n_pad // tn),
        in_specs=[pl.BlockSpec((tm, k), lambda i, j: (i, 0)),
                  pl.BlockSpec((k, tn), lambda i, j: (0, j))],
        out_specs=pl.BlockSpec((tm, tn), lambda i, j: (i, j)),
        out_shape=jax.ShapeDtypeStruct((m_pad, n_pad), out_dtype),
        compiler_params=_cparams(2),
        name="proj_matmul",
    )(a.astype(BF16), b if cast_b_in_kernel else b.astype(BF16))
    if m_pad != m or n_pad != n:
        out = out[:m, :n]
    return out


def _bmm_body(a_ref, b_ref, o_ref):
    o_ref[0] = jnp.dot(a_ref[...], b_ref[0], preferred_element_type=F32).astype(o_ref.dtype)


def pallas_bmm(a, x, out_dtype=F32):
    m, k = a.shape
    bsz, k2, n = x.shape
    assert k == k2
    tm = _pick_tile(m, (1024, 512, 256, 128) if k <= 4096 else (512, 256, 128))
    tn = _pick_tile(n, (512, 256, 128))
    return pl.pallas_call(
        _bmm_body,
        grid=(bsz, m // tm, n // tn),
        in_specs=[pl.BlockSpec((tm, k), lambda bi, i, j: (i, 0)),
                  pl.BlockSpec((1, k, tn), lambda bi, i, j: (bi, 0, j))],
        out_specs=pl.BlockSpec((1, tm, tn), lambda bi, i, j: (bi, i, j)),
        out_shape=jax.ShapeDtypeStruct((bsz, m, n), out_dtype),
        compiler_params=_cparams(3),
        name="dft_matmul",
    )(a.astype(BF16), x.astype(BF16))


def mm(a, b, out_dtype=F32):
    lead = a.shape[:-1]
    out = pallas_matmul(a.reshape(-1, a.shape[-1]), b, out_dtype)
    return out.reshape(*lead, b.shape[-1])


NORM_ROWS = 256


def _norm_mod_body(x_ref, g_ref, sc_ref, sh_ref, o_ref):
    x = x_ref[...]
    y = x * lax.rsqrt(jnp.mean(x * x, axis=-1, keepdims=True) + EPS) * g_ref[...]
    o_ref[...] = (y * (1.0 + sc_ref[0]) + sh_ref[0]).astype(o_ref.dtype)


def norm_mod(x, g, scale, shift, out_dtype):
    b, l, d = x.shape
    tr = NORM_ROWS
    assert l % tr == 0
    per_b = l // tr
    out = pl.pallas_call(
        _norm_mod_body,
        grid=(b * per_b,),
        in_specs=[pl.BlockSpec((tr, d), lambda i: (i, 0)),
                  pl.BlockSpec((1, d), lambda i: (0, 0)),
                  pl.BlockSpec((1, 1, d), lambda i: (i // per_b, 0, 0)),
                  pl.BlockSpec((1, 1, d), lambda i: (i // per_b, 0, 0))],
        out_specs=pl.BlockSpec((tr, d), lambda i: (i, 0)),
        out_shape=jax.ShapeDtypeStruct((b * l, d), out_dtype),
        compiler_params=_cparams(1),
        name="norm_mod",
    )(x.reshape(b * l, d).astype(F32), g.astype(F32).reshape(1, d),
      scale.astype(F32).reshape(b, 1, d), shift.astype(F32).reshape(b, 1, d))
    return out.reshape(b, l, d)


def repack_w_in(w_in):
    wbig = jnp.concatenate([w_in[:, :ORIG_IF], w_in[:, ORIG_IF + N_IF:ORIG_DA]], axis=1)
    wsmall = jnp.concatenate([w_in[:, ORIG_IF:ORIG_IF + N_IF], w_in[:, ORIG_DA:ORIG_DA + N_DA]], axis=1)
    wsmall = jnp.pad(wsmall, ((0, 0), (0, ZS_W - N_IF - N_DA)))
    return wbig.astype(BF16), wsmall.astype(BF16)


def short_conv(x, w, b):
    xp = jnp.pad(x, ((0, 0), (1, 1), (0, 0)))
    return xp[:, :-2] * w[0] + xp[:, 1:-1] * w[1] + xp[:, 2:] * w[2] + b


DFT_BLK = 64


def dft_cos_sin(n_rows, n_cols, modulus):
    s = jnp.arange(n_cols, dtype=jnp.int32)[None, :]
    f0 = jnp.arange(DFT_BLK, dtype=jnp.int32)[:, None]
    f1 = jnp.arange(n_rows // DFT_BLK, dtype=jnp.int32)[:, None] * DFT_BLK

    def tab(f):
        ang = ((f * s) % modulus).astype(F32) * (2.0 * math.pi / modulus)
        return jnp.cos(ang), jnp.sin(ang)
    ca, sa = tab(f1)
    cb, sb = tab(f0)
    c = ca[:, None, :] * cb[None] - sa[:, None, :] * sb[None]
    sn = sa[:, None, :] * cb[None] + ca[:, None, :] * sb[None]
    return c.reshape(n_rows, n_cols), sn.reshape(n_rows, n_cols)


def fourier_mats(l):
    c, sn = dft_cos_sin(l, l, l)
    gf = jnp.concatenate([c, -sn], axis=1) * ((l * FNET_GW) ** -0.5)
    cg, sg = dft_cos_sin(FNET_GW, FNET_GW, FNET_GW)
    eye = jnp.eye(FNET_GROUPS, dtype=F32)
    bd = jnp.concatenate([jnp.kron(eye, cg), jnp.kron(eye, sg)], axis=1)
    return gf.astype(BF16), bd.astype(BF16)


def fourier_mix(u, w, mats):
    gf, bd = mats
    b, l, _ = u.shape
    p = mm(u, bd, BF16)
    pq = jnp.concatenate([p[..., :W], p[..., W:]], axis=1)
    return mm(pallas_bmm(gf, pq, BF16), w)


def hyena_mats(l):
    c, sn = dft_cos_sin(l, l, 2 * l)
    alt = jnp.where(jnp.arange(l) % 2 == 0, 1.0, -1.0).astype(F32)
    g = jnp.concatenate([c, sn.at[0].set(alt)], axis=0)
    return g.astype(BF16), g.T.astype(BF16), alt


def hyena_filter_coeffs(l, p, mats):
    g, _, alt = mats
    t = jnp.linspace(0.0, 1.0, l, dtype=F32)[:, None]
    bands = jnp.linspace(1e-4, HYENA_BANDS - 1, HYENA_BANDS, dtype=F32)[None, :]
    ang = (2.0 * math.pi / l) * jnp.arange(l, dtype=F32)[:, None] * bands
    feats = jnp.concatenate([t, jnp.cos(ang), -jnp.sin(ang)], axis=-1)
    fr = p["hy_f_freq"].astype(F32)
    hid = jnp.sin(fr[0] * (feats @ p["hy_f_w1"].astype(F32) + p["hy_f_b1"].astype(F32)))
    hid = jnp.sin(fr[1] * (hid @ p["hy_f_w2"].astype(F32) + p["hy_f_b2"].astype(F32)))
    filt = (hid @ p["hy_f_w3"].astype(F32)).reshape(l, HYENA_ORDER, N_DIR, GROUP_W)
    deltas = jnp.abs(jnp.linspace(HYENA_SLOW_DECAY, HYENA_FAST_DECAY, GROUP_W, dtype=F32))
    filt = filt * jnp.exp(-t * deltas)[:, None, None, :]
    fwd, bwd = filt[:, :, 0], filt[:, :, 1]
    bwd = bwd.at[0].set(0.0)
    norm = jnp.sum(jnp.abs(fwd), axis=0, keepdims=True) + jnp.sum(jnp.abs(bwd), axis=0, keepdims=True) + EPS
    n_col = HYENA_ORDER * GROUP_W
    fp = ((fwd + bwd) / norm).reshape(l, n_col)
    fm = ((fwd - bwd) / norm).reshape(l, n_col)
    kre = pallas_matmul(g[:l], fp)
    kim = -pallas_matmul(g[l:], fm)
    k_nyq = jnp.sum(alt[:, None] * fp, axis=0)
    row0 = (jnp.arange(l) == 0)[:, None]
    kim = jnp.where(row0, 0.0, kim)
    a4 = jnp.where(row0, k_nyq[None, :], kre)
    coef = jnp.where(row0, 0.5 / l, 1.0 / l)
    return tuple((coef * a).reshape(l, HYENA_ORDER, GROUP_W) for a in (kre, kim, a4))


def long_conv(z, mats, ck):
    g, gt, _ = mats
    l = z.shape[1]
    zhat = pallas_bmm(g, z)
    zc, zs = zhat[:, :l], zhat[:, l:]
    kre, kim, a4 = ck
    y = jnp.concatenate([zc * kre + zs * kim, zs * a4 - zc * kim], axis=1)
    return pallas_bmm(gt, y)


def hyena_mix(u, conv_w, conv_b, coeffs, bias, mats):
    u = short_conv(u, conv_w, conv_b).astype(F32)
    v, x1, x2 = jnp.split(u, 3, axis=-1)
    bias = bias.astype(F32)
    z = x1 * (long_conv(v, mats, tuple(a[:, 0] for a in coeffs)) + bias[0] * v)
    return x2 * (long_conv(z, mats, tuple(a[:, 1] for a in coeffs)) + bias[1] * z)


def gla_out(o, g):
    b, l, _ = o.shape
    oh = o.reshape(b, l, GLA_HEADS, GLA_DV)
    oh = oh * lax.rsqrt(jnp.mean(oh * oh, axis=-1, keepdims=True) + EPS)
    return oh.reshape(b, l, GROUP_W) * g.astype(F32)


def zcol(zb, name, width):
    return zb[..., OFF[name]:OFF[name] + width].astype(F32)


def combine(zb, o_m, o_g, p, mats):
    l = zb.shape[1]
    f_mats, h_mats = mats

    def silu(a):
        return jax.nn.silu(a)
    y_a = fourier_mix(zcol(zb, "a_u", W), p["fnet_w"], f_mats) * silu(zcol(zb, "a_g", W))
    y_b = hyena_mix(zcol(zb, "b_u", 3 * W), p["hy_conv_w"], p["hy_conv_b"], hyena_filter_coeffs(l, p, h_mats),
                    p["hy_bias"], h_mats) * silu(zcol(zb, "b_g", W))
    y_c = o_m * jax.nn.sigmoid(zcol(zb, "c_o", W)) * silu(zcol(zb, "c_g", W))
    y_d = gla_out(o_g, p["gla_norm_g"]) * silu(zcol(zb, "d_g", W))
    y = jnp.concatenate([y_a, y_b, y_c, y_d], axis=-1)
    return mm(y, p["w_out"])


def mixer_layer(x, xc, c, c_ctx, p, last, mats, mats_c):
    b, l, d = x.shape
    mod3 = pallas_matmul(jnp.concatenate([jax.nn.silu(c), jax.nn.silu(c_ctx)[None]], axis=0), p["ada_w"],
                         cast_b_in_kernel=True) + p["ada_b"]
    shift, scale, gate = jnp.split(mod3[:b], 3, axis=-1)
    mod_c = mod3[b]
    h = norm_mod(x, p["norm_g"], scale, shift, BF16)
    hc = norm_mod(xc, p["norm_g"], jnp.broadcast_to(mod_c[d:2 * d], (b, d)), jnp.broadcast_to(mod_c[:d], (b, d)), BF16)
    wbig, wsmall = repack_w_in(p["w_in"])
    zb = mm(h, wbig, BF16)
    zs = mm(h, wsmall)
    zcb = mm(hc, wbig, BF16)
    zcs = mm(hc, wsmall)

    cmf, cmb, m_states = mlstm_scan(zcb, zcs, p["ml_conv_w"], p["ml_conv_b"], p["ml_gate_b"],
                                    mlstm_zero_state(b), col_major=False)
    lmf, lmb, _ = mlstm_scan(zb, zs, p["ml_conv_w"], p["ml_conv_b"], p["ml_gate_b"], m_states, col_major=True)
    lat_m = lmf.astype(F32) + lmb.astype(F32)

    cgf, cgb, g_states = gla_scan(zcb, zcs, p["gla_w2"], p["gla_b2"], gla_zero_state(b))
    lgf, lgb, _ = gla_scan(zb, zs, p["gla_w2"], p["gla_b2"], g_states)
    lat_g = lgf.astype(F32) + lgb.astype(F32)

    x = x + gate[:, None] * combine(zb, lat_m, lat_g, p, mats)
    if not last:
        ctx_m = cmf.astype(F32) + cmb.astype(F32)
        ctx_g = cgf.astype(F32) + cgb.astype(F32)
        xc = xc + mod_c[2 * d:] * combine(zcb, ctx_m, ctx_g, p, mats_c)
    return x, xc


def kernel(x, c, ctx, c_ctx, ada_w, ada_b, norm_g, w_in, fnet_w, hy_conv_w, hy_conv_b, hy_f_w1, hy_f_b1,
           hy_f_w2, hy_f_b2, hy_f_w3, hy_f_freq, hy_bias, ml_conv_w, ml_conv_b, ml_gate_b, gla_w2, gla_b2,
           gla_norm_g, w_out, final_g):
    xc = ctx
    depth = ada_w.shape[0]
    mats = (fourier_mats(x.shape[1]), hyena_mats(x.shape[1]))
    mats_c = (fourier_mats(ctx.shape[1]), hyena_mats(ctx.shape[1]))
    for i in range(depth):
        p = {
            "ada_w": ada_w[i], "ada_b": ada_b[i], "norm_g": norm_g[i], "w_in": w_in[i], "fnet_w": fnet_w[i],
            "hy_conv_w": hy_conv_w[i], "hy_conv_b": hy_conv_b[i], "hy_f_w1": hy_f_w1[i], "hy_f_b1": hy_f_b1[i],
            "hy_f_w2": hy_f_w2[i], "hy_f_b2": hy_f_b2[i], "hy_f_w3": hy_f_w3[i], "hy_f_freq": hy_f_freq[i],
            "hy_bias": hy_bias[i], "ml_conv_w": ml_conv_w[i], "ml_conv_b": ml_conv_b[i], "ml_gate_b": ml_gate_b[i],
            "gla_w2": gla_w2[i], "gla_b2": gla_b2[i], "gla_norm_g": gla_norm_g[i], "w_out": w_out[i],
        }
        x, xc = mixer_layer(x, xc, c, c_ctx, p, i == depth - 1, mats, mats_c)
    zeros = jnp.zeros((x.shape[0], x.shape[2]), F32)
    return norm_mod(x, final_g, zeros, zeros, x.dtype)
```
